```python
import jax, jax.numpy as jnp
from jax import lax
import numpy as np

D_MODEL = 1024
BATCH = 8
SEQ = 4096
DEPTH = 2

N_MEM = 256
HEAD_DIM = 64
N_Q_HEADS = 8
N_KV_HEADS = 2
ATTN_WIDTH = N_Q_HEADS * HEAD_DIM
KV_WIDTH = N_KV_HEADS * HEAD_DIM
CONV_CH = D_MODEL - ATTN_WIDTH
MIX_WIDTH = ATTN_WIDTH + CONV_CH
IN_COLS = ATTN_WIDTH + 2 * KV_WIDTH + 2 * CONV_CH
CONV_K = 31
WINDOW = 128
BLOCK = 128
N_X_HEADS = 4
X_HEAD_DIM = D_MODEL // N_X_HEADS
D_FF = (((8 * D_MODEL + 2) // 3 + 255) // 256) * 256
EPS = 1e-6
NEG = -1e30

kernel_name = "hybrid_swa_sink_conformer_memxattn"


def rmsnorm(x, g):
    xf = x.astype(jnp.float32)
    y = xf * lax.rsqrt(jnp.mean(xf * xf, axis=-1, keepdims=True) + EPS)
    return (y * g.astype(jnp.float32)).astype(x.dtype)


def layernorm(x, g, b):
    xf = x.astype(jnp.float32)
    mu = jnp.mean(xf, axis=-1, keepdims=True)
    xc = xf - mu
    y = xc * lax.rsqrt(jnp.mean(xc * xc, axis=-1, keepdims=True) + EPS)
    return (y * g.astype(jnp.float32) + b.astype(jnp.float32)).astype(x.dtype)


def alibi_slopes(n):
    return 2.0 ** (-8.0 * (jnp.arange(n, dtype=jnp.float32) + 1.0) / n)


def sliding_window_attention(q, k, v, sinks):
    B, S, H, hd = q.shape
    G = H // N_KV_HEADS
    nb = S // BLOCK
    qb = q.reshape(B, nb, BLOCK, N_KV_HEADS, G, hd)
    kb = k.reshape(B, nb, BLOCK, N_KV_HEADS, hd)
    vb = v.reshape(B, nb, BLOCK, N_KV_HEADS, hd)
    pad = ((0, 0), (1, 0), (0, 0), (0, 0), (0, 0))
    kk = jnp.concatenate([jnp.pad(kb[:, :-1], pad), kb], axis=2)
    vv = jnp.concatenate([jnp.pad(vb[:, :-1], pad), vb], axis=2)
    scores = jnp.einsum('bnqkgd,bnskd->bnkgqs', qb, kk).astype(jnp.float32) * (hd ** -0.5)
    dist = (jnp.arange(BLOCK)[:, None] + BLOCK - jnp.arange(2 * BLOCK)[None, :])
    valid = (dist >= 0) & (dist < WINDOW)
    exists = (jnp.arange(nb)[:, None, None] > 0) | (jnp.arange(2 * BLOCK)[None, None, :] >= BLOCK)
    valid = valid[None] & exists
    slopes = alibi_slopes(H).reshape(N_KV_HEADS, G)
    bias = -slopes[:, :, None, None] * dist.astype(jnp.float32)
    scores = jnp.where(valid[None, :, None, None], scores + bias, NEG)
    sink = sinks.astype(jnp.float32).reshape(N_KV_HEADS, G)[None, None, :, :, None, None]
    m = jnp.maximum(jnp.max(scores, axis=-1, keepdims=True), sink)
    p = jnp.exp(scores - m)
    probs = p / (jnp.sum(p, axis=-1, keepdims=True) + jnp.exp(sink - m))
    out = jnp.einsum('bnkgqs,bnskd->bnqkgd', probs.astype(v.dtype), vv)
    return out.reshape(B, S, H * hd)


def conformer_conv(c, conv_w, conv_b, ln_g, ln_b):
    val, gate = jnp.split(c, 2, axis=-1)
    g = val * jax.nn.sigmoid(gate)
    y = lax.conv_general_dilated(
        g, conv_w[:, None, :].astype(g.dtype), window_strides=(1,),
        padding=[(CONV_K - 1, 0)], dimension_numbers=('NWC', 'WIO', 'NWC'),
        feature_group_count=CONV_CH) + conv_b
    y = layernorm(y, ln_g, ln_b)
    return jax.nn.silu(y)


def memory_cross_attention(h, mem_n, wq, wkv, qg, kg, wo):
    B, S, _ = h.shape
    q = jnp.einsum('bsd,de->bse', h, wq).reshape(B, S, N_X_HEADS, X_HEAD_DIM)
    kv = jnp.einsum('bmd,de->bme', mem_n, wkv)
    k, v = jnp.split(kv, 2, axis=-1)
    k = k.reshape(B, -1, N_X_HEADS, X_HEAD_DIM)
    v = v.reshape(B, -1, N_X_HEADS, X_HEAD_DIM)
    q = rmsnorm(q, qg)
    k = rmsnorm(k, kg)
    s = jnp.einsum('bshd,bmhd->bhsm', q, k).astype(jnp.float32) * (X_HEAD_DIM ** -0.5)
    p = jax.nn.softmax(s, axis=-1).astype(v.dtype)
    o = jnp.einsum('bhsm,bmhd->bshd', p, v).reshape(B, S, D_MODEL)
    return jnp.einsum('bse,ed->bsd', o, wo)


def swiglu(h, w_gate_up, w_down):
    gu = jnp.einsum('bsd,df->bsf', h, w_gate_up)
    g, u = jnp.split(gu, 2, axis=-1)
    return jnp.einsum('bsf,fd->bsd', jax.nn.silu(g) * u, w_down)


def setup_inputs(seed: int = 0) -> dict:
    key = jax.random.key(seed)
    ks = jax.random.split(key, 24)
    f = jnp.float32
    nrm = lambda k, shape, scale: (jax.random.normal(k, shape, f) * scale).astype(f)
    gain = lambda k, shape: (1.0 + 0.02 * jax.random.normal(k, shape, f)).astype(f)
    L = DEPTH
    return {
        "x": jax.random.normal(ks[0], (BATCH, SEQ, D_MODEL), f),
        "mem": jax.random.normal(ks[1], (BATCH, N_MEM, D_MODEL), f),
        "norm_mix_g": gain(ks[2], (L, D_MODEL)),
        "w_in": nrm(ks[3], (L, D_MODEL, IN_COLS), D_MODEL ** -0.5),
        "q_norm_g": gain(ks[4], (L, HEAD_DIM)),
        "k_norm_g": gain(ks[5], (L, HEAD_DIM)),
        "sinks": nrm(ks[6], (L, N_Q_HEADS), 0.5),
        "conv_w": nrm(ks[7], (L, CONV_K, CONV_CH), CONV_K ** -0.5),
        "conv_b": nrm(ks[8], (L, CONV_CH), 0.01),
        "conv_ln_g": gain(ks[9], (L, CONV_CH)),
        "conv_ln_b": nrm(ks[10], (L, CONV_CH), 0.01),
        "w_out": nrm(ks[11], (L, MIX_WIDTH, D_MODEL), MIX_WIDTH ** -0.5),
        "norm_x_g": gain(ks[12], (L, D_MODEL)),
        "norm_mem_g": gain(ks[13], (L, D_MODEL)),
        "wq_x": nrm(ks[14], (L, D_MODEL, D_MODEL), D_MODEL ** -0.5),
        "wkv_x": nrm(ks[15], (L, D_MODEL, 2 * D_MODEL), D_MODEL ** -0.5),
        "xq_norm_g": gain(ks[16], (L, X_HEAD_DIM)),
        "xk_norm_g": gain(ks[17], (L, X_HEAD_DIM)),
        "wo_x": nrm(ks[18], (L, D_MODEL, D_MODEL), D_MODEL ** -0.5),
        "norm_ffn_g": gain(ks[19], (L, D_MODEL)),
        "w_gate_up": nrm(ks[20], (L, D_MODEL, 2 * D_FF), D_MODEL ** -0.5),
        "w_down": nrm(ks[21], (L, D_FF, D_MODEL), D_FF ** -0.5),
    }


def reference(x, mem, norm_mix_g, w_in, q_norm_g, k_norm_g, sinks, conv_w, conv_b,
              conv_ln_g, conv_ln_b, w_out, norm_x_g, norm_mem_g, wq_x, wkv_x,
              xq_norm_g, xk_norm_g, wo_x, norm_ffn_g, w_gate_up, w_down):
    B, S, _ = x.shape
    for l in range(DEPTH):
        h = rmsnorm(x, norm_mix_g[l])
        u = jnp.einsum('bsd,dp->bsp', h, w_in[l])
        q = u[..., :ATTN_WIDTH].reshape(B, S, N_Q_HEADS, HEAD_DIM)
        k = u[..., ATTN_WIDTH:ATTN_WIDTH + KV_WIDTH].reshape(B, S, N_KV_HEADS, HEAD_DIM)
        v = u[..., ATTN_WIDTH + KV_WIDTH:ATTN_WIDTH + 2 * KV_WIDTH].reshape(B, S, N_KV_HEADS, HEAD_DIM)
        c = u[..., ATTN_WIDTH + 2 * KV_WIDTH:]
        q = rmsnorm(q, q_norm_g[l])
        k = rmsnorm(k, k_norm_g[l])
        attn = sliding_window_attention(q, k, v, sinks[l])
        conv = conformer_conv(c, conv_w[l], conv_b[l], conv_ln_g[l], conv_ln_b[l])
        mixed = jnp.concatenate([attn, conv], axis=-1)
        x = x + jnp.einsum('bse,ed->bsd', mixed, w_out[l])
        hx = rmsnorm(x, norm_x_g[l])
        mem_n = rmsnorm(mem, norm_mem_g[l])
        x = x + memory_cross_attention(hx, mem_n, wq_x[l], wkv_x[l], xq_norm_g[l], xk_norm_g[l], wo_x[l])
        hf = rmsnorm(x, norm_ffn_g[l])
        x = x + swiglu(hf, w_gate_up[l], w_down[l])
    return x
```

```python
import functools

import jax
import jax.numpy as jnp
import numpy as np
from jax import lax
from jax.experimental import pallas as pl
from jax.experimental.pallas import tpu as pltpu

D_MODEL = 1024
N_MEM = 256
HEAD_DIM = 64
N_Q_HEADS = 8
N_KV_HEADS = 2
Q_PER_KV = N_Q_HEADS // N_KV_HEADS
ATTN_WIDTH = N_Q_HEADS * HEAD_DIM
KV_WIDTH = N_KV_HEADS * HEAD_DIM
CONV_CH = D_MODEL - ATTN_WIDTH
CONV_K = 31
BLOCK = 128
N_X_HEADS = 4
X_HEAD_DIM = D_MODEL // N_X_HEADS
D_FF = 2816
EPS = 1e-6
NEG = -1e30

CONV_HALO = 32
MIX_TILE = 256
XATTN_TILE = 512
FFN_TILE = 512
VMEM_LIMIT_BYTES = 56 * 1024 * 1024

BF16 = jnp.bfloat16
F32 = jnp.float32


def _dot(a, b):
    return jnp.dot(a, b, preferred_element_type=F32)


def _dot_nt(a, b):
    return lax.dot_general(a, b, (((1,), (1,)), ((), ())), preferred_element_type=F32)


def _rms(x, g):
    return x * lax.rsqrt(jnp.mean(x * x, axis=-1, keepdims=True) + EPS) * g


def _sigmoid(x):
    return 1.0 / (1.0 + jnp.exp(-x))


def _group_sumsq(x, ones_blockdiag):
    sq = x * x
    hi = sq.astype(BF16)
    lo = (sq - hi.astype(F32)).astype(BF16)
    return _dot(hi, ones_blockdiag) + _dot(lo, ones_blockdiag)


def _mixer_kernel(x_ref, gmix_ref, wq_ref, wkv_ref, wc_ref, ones_ref, gq_ref, gk_ref,
                  bias_ref, sink_ref, cw_ref, cb_ref, lng_ref, lnb_ref, wout_ref,
                  o_ref, kprev_ref, vprev_ref, gbuf_ref, mixed_ref):
    tile = x_ref.shape[1]
    n_blocks = tile // BLOCK
    step = pl.program_id(1)

    @pl.when(step == 0)
    def _():
        kprev_ref[...] = jnp.zeros_like(kprev_ref)
        vprev_ref[...] = jnp.zeros_like(vprev_ref)
        gbuf_ref[0:CONV_HALO, :] = jnp.zeros((CONV_HALO, CONV_CH), F32)

    x = x_ref[0]
    h = _rms(x, gmix_ref[...]).astype(BF16)

    q = _dot(h, wq_ref[...])
    kv = _dot(h, wkv_ref[...])
    k = kv[:, :KV_WIDTH]
    v = kv[:, KV_WIDTH:].astype(BF16)
    ones = ones_ref[...]
    qn = q * lax.rsqrt(_group_sumsq(q, ones) * (1.0 / HEAD_DIM) + EPS) * gq_ref[...]
    kn = k * lax.rsqrt(_group_sumsq(k, ones[:KV_WIDTH, :KV_WIDTH]) * (1.0 / HEAD_DIM) + EPS) * gk_ref[...]
    qn = qn.astype(BF16)
    kn = kn.astype(BF16)

    lane = lax.broadcasted_iota(jnp.int32, (2 * BLOCK, KV_WIDTH), 1)
    first_half = lane < HEAD_DIM
    lane_o = lax.broadcasted_iota(jnp.int32, (Q_PER_KV * BLOCK, KV_WIDTH), 1)
    zero = jnp.zeros((2 * BLOCK, KV_WIDTH), BF16)
    sink = sink_ref[...]
    for i in range(n_blocks):
        rows = slice(i * BLOCK, (i + 1) * BLOCK)
        if i == 0:
            k_prev, v_prev = kprev_ref[...], vprev_ref[...]
            bias = bias_ref[jnp.where(step == 0, 1, 0)]
        else:
            prev = slice((i - 1) * BLOCK, i * BLOCK)
            k_prev, v_prev = kn[prev], v[prev]
            bias = bias_ref[0]
        kk = jnp.concatenate([k_prev, kn[rows]], axis=0)
        vv = jnp.concatenate([v_prev, v[rows]], axis=0)
        k_st = jnp.concatenate([jnp.where(first_half, kk, zero), jnp.where(first_half, zero, kk)], axis=0)
        v_st = jnp.concatenate([jnp.where(first_half, vv, zero), jnp.where(first_half, zero, vv)], axis=0)
        q_st = jnp.concatenate([qn[rows, j * KV_WIDTH:(j + 1) * KV_WIDTH] for j in range(Q_PER_KV)], axis=0)
        s = _dot_nt(q_st, k_st) * (HEAD_DIM ** -0.5) + bias
        probs, inv = [], []
        for c in range(N_KV_HEADS):
            sc = s[:, c * 2 * BLOCK:(c + 1) * 2 * BLOCK]
            snk = sink[:, c:c + 1]
            m = jnp.maximum(jnp.max(sc, axis=-1, keepdims=True), snk)
            p = jnp.exp(sc - m)
            inv.append(1.0 / (jnp.sum(p, axis=-1, keepdims=True) + jnp.exp(snk - m)))
            probs.append(p.astype(BF16))
        o = _dot(jnp.concatenate(probs, axis=1), v_st)
        o = o * jnp.where(lane_o < HEAD_DIM, inv[0], inv[1])
        for j in range(Q_PER_KV):
            mixed_ref[rows, j * KV_WIDTH:(j + 1) * KV_WIDTH] = o[j * BLOCK:(j + 1) * BLOCK].astype(BF16)
    kprev_ref[...] = kn[tile - BLOCK:]
    vprev_ref[...] = v[tile - BLOCK:]

    val = _dot(h, wc_ref[:, :CONV_CH])
    gate = _dot(h, wc_ref[:, CONV_CH:])
    gbuf_ref[CONV_HALO:CONV_HALO + tile, :] = val * _sigmoid(gate)
    y = jnp.broadcast_to(cb_ref[...], (tile, CONV_CH))
    for j in range(CONV_K):
        off = CONV_HALO - (CONV_K - 1) + j
        y = y + cw_ref[j:j + 1, :] * gbuf_ref[off:off + tile, :]
    gbuf_ref[0:CONV_HALO, :] = gbuf_ref[tile:tile + CONV_HALO, :]
    mu = jnp.mean(y, axis=-1, keepdims=True)
    yc = y - mu
    yn = yc * lax.rsqrt(jnp.mean(yc * yc, axis=-1, keepdims=True) + EPS) * lng_ref[...] + lnb_ref[...]
    mixed_ref[:, ATTN_WIDTH:] = (yn * _sigmoid(yn)).astype(BF16)

    o_ref[0] = x + _dot(mixed_ref[...], wout_ref[...])


def _const_spec(shape):
    return pl.BlockSpec(shape, lambda *_: (0,) * len(shape))


def _mixer(x, p):
    batch, seq, _ = x.shape
    tile = MIX_TILE
    tok_spec = pl.BlockSpec((1, tile, D_MODEL), lambda b, s: (b, s, 0))
    consts = [p["gmix"], p["wq"], p["wkv"], p["wc"], p["ones"], p["gq"], p["gk"], p["bias"], p["sink"],
              p["cw"], p["cb"], p["lng"], p["lnb"], p["wout"]]
    return pl.pallas_call(
        _mixer_kernel,
        grid=(batch, seq // tile),
        in_specs=[tok_spec] + [_const_spec(c.shape) for c in consts],
        out_specs=tok_spec,
        out_shape=jax.ShapeDtypeStruct(x.shape, F32),
        scratch_shapes=[
            pltpu.VMEM((BLOCK, KV_WIDTH), BF16),
            pltpu.VMEM((BLOCK, KV_WIDTH), BF16),
            pltpu.VMEM((tile + CONV_HALO, CONV_CH), F32),
            pltpu.VMEM((tile, D_MODEL), BF16),
        ],
        compiler_params=pltpu.CompilerParams(
            dimension_semantics=("arbitrary", "arbitrary"), vmem_limit_bytes=VMEM_LIMIT_BYTES),
        name="mixer",
    )(x, *consts)


def _memkv_kernel(mem_ref, gmem_ref, wkv_ref, gk_ref, kt_ref, v_ref):
    mem_n = _rms(mem_ref[0], gmem_ref[...]).astype(BF16)
    kv = _dot(mem_n, wkv_ref[...])
    for hd in range(N_X_HEADS):
        cols = slice(hd * X_HEAD_DIM, (hd + 1) * X_HEAD_DIM)
        kt_ref[0, cols, :] = _rms(kv[:, cols], gk_ref[...]).T.astype(BF16)
    v_ref[0] = kv[:, D_MODEL:].astype(BF16)


def _memkv(mem, p):
    batch = mem.shape[0]
    consts = [p["gmem"], p["wkv_x"], p["gxk"]]
    return pl.pallas_call(
        _memkv_kernel,
        grid=(batch,),
        in_specs=[pl.BlockSpec((1, N_MEM, D_MODEL), lambda b: (b, 0, 0))] + [_const_spec(c.shape) for c in consts],
        out_specs=[pl.BlockSpec((1, D_MODEL, N_MEM), lambda b: (b, 0, 0)),
                   pl.BlockSpec((1, N_MEM, D_MODEL), lambda b: (b, 0, 0))],
        out_shape=[jax.ShapeDtypeStruct((batch, D_MODEL, N_MEM), BF16),
                   jax.ShapeDtypeStruct((batch, N_MEM, D_MODEL), BF16)],
        compiler_params=pltpu.CompilerParams(
            dimension_semantics=("arbitrary",), vmem_limit_bytes=VMEM_LIMIT_BYTES),
        name="memkv",
    )(mem, *consts)


def _xattn_kernel(x_ref, gx_ref, wq_ref, gq_ref, kt_ref, v_ref, wo_ref, o_ref, att_ref):
    x = x_ref[0]
    h = _rms(x, gx_ref[...]).astype(BF16)
    q = _dot(h, wq_ref[...])
    for hd in range(N_X_HEADS):
        cols = slice(hd * X_HEAD_DIM, (hd + 1) * X_HEAD_DIM)
        qn = _rms(q[:, cols], gq_ref[...]).astype(BF16)
        s = _dot(qn, kt_ref[0, cols, :]) * (X_HEAD_DIM ** -0.5)
        p = jnp.exp(s - jnp.max(s, axis=-1, keepdims=True))
        inv = 1.0 / jnp.sum(p, axis=-1, keepdims=True)
        att_ref[:, cols] = (_dot(p.astype(BF16), v_ref[0, :, cols]) * inv).astype(BF16)
    o_ref[0] = x + _dot(att_ref[...], wo_ref[...])


def _xattn(x, kt, v, p):
    batch, seq, _ = x.shape
    tile = XATTN_TILE
    tok_spec = pl.BlockSpec((1, tile, D_MODEL), lambda b, s: (b, s, 0))
    return pl.pallas_call(
        _xattn_kernel,
        grid=(batch, seq // tile),
        in_specs=[tok_spec, _const_spec(p["gx"].shape), _const_spec(p["wq_x"].shape), _const_spec(p["gxq"].shape),
                  pl.BlockSpec((1, D_MODEL, N_MEM), lambda b, s: (b, 0, 0)),
                  pl.BlockSpec((1, N_MEM, D_MODEL), lambda b, s: (b, 0, 0)),
                  _const_spec(p["wo_x"].shape)],
        out_specs=tok_spec,
        out_shape=jax.ShapeDtypeStruct(x.shape, F32),
        scratch_shapes=[pltpu.VMEM((tile, D_MODEL), BF16)],
        compiler_params=pltpu.CompilerParams(
            dimension_semantics=("arbitrary", "arbitrary"), vmem_limit_bytes=VMEM_LIMIT_BYTES),
        name="xattn",
    )(x, p["gx"], p["wq_x"], p["gxq"], kt, v, p["wo_x"])


def _ffn_kernel(x_ref, g_ref, wg_ref, wu_ref, wd_ref, o_ref):
    x = x_ref[...]
    h = _rms(x, g_ref[...]).astype(BF16)
    gate = _dot(h, wg_ref[...])
    up = _dot(h, wu_ref[...])
    act = (gate * _sigmoid(gate) * up).astype(BF16)
    o_ref[...] = x + _dot(act, wd_ref[...])


def _ffn(x, p):
    batch, seq, _ = x.shape
    tile = FFN_TILE
    xf = x.reshape(batch * seq, D_MODEL)
    tok_spec = pl.BlockSpec((tile, D_MODEL), lambda t: (t, 0))
    single = dict(pipeline_mode=pl.Buffered(1))
    out = pl.pallas_call(
        _ffn_kernel,
        grid=(batch * seq // tile,),
        in_specs=[tok_spec, _const_spec(p["gffn"].shape),
                  pl.BlockSpec(p["wg"].shape, lambda t: (0, 0), **single),
                  pl.BlockSpec(p["wu"].shape, lambda t: (0, 0), **single),
                  pl.BlockSpec(p["wd"].shape, lambda t: (0, 0), **single)],
        out_specs=tok_spec,
        out_shape=jax.ShapeDtypeStruct(xf.shape, F32),
        compiler_params=pltpu.CompilerParams(
            dimension_semantics=("arbitrary",), vmem_limit_bytes=VMEM_LIMIT_BYTES),
        name="ffn",
    )(xf, p["gffn"], p["wg"], p["wu"], p["wd"])
    return out.reshape(x.shape)


def _q_head_order():
    cols = []
    for j in range(Q_PER_KV):
        for c in range(N_KV_HEADS):
            hd = j + Q_PER_KV * c
            cols.extend(range(hd * HEAD_DIM, (hd + 1) * HEAD_DIM))
    return np.asarray(cols, dtype=np.int32)


def _swa_bias():
    qi = np.arange(BLOCK)[:, None]
    sk = np.arange(2 * BLOCK)[None, :]
    dist = qi + BLOCK - sk
    valid = (dist >= 0) & (dist < BLOCK)
    slopes = 2.0 ** (-8.0 * (np.arange(N_Q_HEADS, dtype=np.float32) + 1.0) / N_Q_HEADS)
    table = np.empty((2, Q_PER_KV * BLOCK, N_KV_HEADS * 2 * BLOCK), np.float32)
    for first in range(2):
        ok = valid & (sk >= BLOCK) if first else valid
        for j in range(Q_PER_KV):
            for c in range(N_KV_HEADS):
                slope = slopes[j + Q_PER_KV * c]
                blk = np.where(ok, -slope * dist.astype(np.float32), np.float32(NEG))
                table[first, j * BLOCK:(j + 1) * BLOCK, c * 2 * BLOCK:(c + 1) * 2 * BLOCK] = blk
    return jnp.asarray(table)


def _layer_params(l, norm_mix_g, w_in, q_norm_g, k_norm_g, sinks, conv_w, conv_b, conv_ln_g, conv_ln_b,
                  w_out, norm_x_g, norm_mem_g, wq_x, wkv_x, xq_norm_g, xk_norm_g, wo_x, norm_ffn_g,
                  w_gate_up, w_down):
    order = _q_head_order()
    row = lambda a: a.reshape(1, -1).astype(F32)
    w_in_l = w_in[l]
    w_out_l = w_out[l]
    head_of_row = np.repeat(np.arange(Q_PER_KV), BLOCK)
    sink_tab = jnp.stack([sinks[l][head_of_row + Q_PER_KV * c] for c in range(N_KV_HEADS)], axis=1)
    grp = np.arange(ATTN_WIDTH) // HEAD_DIM
    return dict(
        gmix=row(norm_mix_g[l]),
        wq=w_in_l[:, :ATTN_WIDTH][:, order].astype(BF16),
        wkv=w_in_l[:, ATTN_WIDTH:ATTN_WIDTH + 2 * KV_WIDTH].astype(BF16),
        wc=w_in_l[:, ATTN_WIDTH + 2 * KV_WIDTH:].astype(BF16),
        ones=jnp.asarray(grp[:, None] == grp[None, :], BF16),
        gq=row(jnp.tile(q_norm_g[l], N_Q_HEADS)),
        gk=row(jnp.tile(k_norm_g[l], N_KV_HEADS)),
        bias=_swa_bias(),
        sink=sink_tab.astype(F32),
        cw=conv_w[l].astype(F32),
        cb=row(conv_b[l]),
        lng=row(conv_ln_g[l]),
        lnb=row(conv_ln_b[l]),
        wout=jnp.concatenate([w_out_l[:ATTN_WIDTH][order], w_out_l[ATTN_WIDTH:]], axis=0).astype(BF16),
        gx=row(norm_x_g[l]),
        gmem=row(norm_mem_g[l]),
        wq_x=wq_x[l].astype(BF16),
        wkv_x=wkv_x[l].astype(BF16),
        gxq=row(xq_norm_g[l]),
        gxk=row(xk_norm_g[l]),
        wo_x=wo_x[l].astype(BF16),
        gffn=row(norm_ffn_g[l]),
        wg=w_gate_up[l][:, :D_FF].astype(BF16),
        wu=w_gate_up[l][:, D_FF:].astype(BF16),
        wd=w_down[l].astype(BF16),
    )


def kernel(x, mem, norm_mix_g, w_in, q_norm_g, k_norm_g, sinks, conv_w, conv_b, conv_ln_g, conv_ln_b, w_out, norm_x_g, norm_mem_g, wq_x, wkv_x, xq_norm_g, xk_norm_g, wo_x, norm_ffn_g, w_gate_up, w_down):
    depth = w_in.shape[0]
    for l in range(depth):
        p = _layer_params(l, norm_mix_g, w_in, q_norm_g, k_norm_g, sinks, conv_w, conv_b, conv_ln_g,
                          conv_ln_b, w_out, norm_x_g, norm_mem_g, wq_x, wkv_x, xq_norm_g, xk_norm_g,
                          wo_x, norm_ffn_g, w_gate_up, w_down)
        x = _mixer(x, p)
        kt, v = _memkv(mem, p)
        x = _xattn(x, kt, v, p)
        x = _ffn(x, p)
    return x
```

```python
import functools

import jax
import jax.numpy as jnp
import numpy as np
from jax import lax
from jax.experimental import pallas as pl
from jax.experimental.pallas import tpu as pltpu

D_MODEL = 1024
N_MEM = 256
HEAD_DIM = 64
N_Q_HEADS = 8
N_KV_HEADS = 2
Q_PER_KV = N_Q_HEADS // N_KV_HEADS
ATTN_WIDTH = N_Q_HEADS * HEAD_DIM
KV_WIDTH = N_KV_HEADS * HEAD_DIM
CONV_CH = D_MODEL - ATTN_WIDTH
CONV_K = 31
BLOCK = 128
N_X_HEADS = 4
X_HEAD_DIM = D_MODEL // N_X_HEADS
D_FF = 2816
EPS = 1e-6
NEG = -1e30

SUBLANES = 8
CONV_HALO = 32
CONV_ROWS = 32
LOG2E = 1.4426950408889634
MIX_TILE = 256
XATTN_TILE = 512
FFN_TILE = 512
VMEM_LIMIT_BYTES = 56 * 1024 * 1024

BF16 = jnp.bfloat16
F32 = jnp.float32


def _dot(a, b):
    return jnp.dot(a, b, preferred_element_type=F32)


def _dot_nt(a, b):
    return lax.dot_general(a, b, (((1,), (1,)), ((), ())), preferred_element_type=F32)


def _rms(x, g):
    return x * lax.rsqrt(jnp.mean(x * x, axis=-1, keepdims=True) + EPS) * g


def _sigmoid(x):
    return 1.0 / (1.0 + jnp.exp(-x))


def _head_pair_rms(x, gain):
    lane = lax.broadcasted_iota(jnp.int32, (x.shape[0], 2 * HEAD_DIM), 1)
    first = lane < HEAD_DIM
    out = []
    for j in range(x.shape[1] // (2 * HEAD_DIM)):
        cols = slice(j * 2 * HEAD_DIM, (j + 1) * 2 * HEAD_DIM)
        blk = x[:, cols]
        sq = blk * blk
        ss_a = jnp.sum(jnp.where(first, sq, 0.0), axis=-1, keepdims=True)
        ss_b = jnp.sum(jnp.where(first, 0.0, sq), axis=-1, keepdims=True)
        inv = jnp.where(first, lax.rsqrt(ss_a * (1.0 / HEAD_DIM) + EPS), lax.rsqrt(ss_b * (1.0 / HEAD_DIM) + EPS))
        out.append(blk * inv * gain[:, cols])
    return out[0] if len(out) == 1 else jnp.concatenate(out, axis=1)


def _mixer_kernel(x_ref, gmix_ref, wq_ref, wkv_ref, wc_ref, gq_ref, gk_ref,
                  bias_ref, cw_ref, cb_ref, lng_ref, lnb_ref, wout_ref,
                  o_ref, kprev_ref, vprev_ref, gsh_ref, mixed_ref):
    tile = x_ref.shape[1]
    n_blocks = tile // BLOCK
    step = pl.program_id(1)

    @pl.when(step == 0)
    def _():
        kprev_ref[...] = jnp.zeros_like(kprev_ref)
        vprev_ref[...] = jnp.zeros_like(vprev_ref)
        gsh_ref[0, 0:CONV_HALO, :] = jnp.zeros((CONV_HALO, CONV_CH), F32)

    x = x_ref[0]
    h = _rms(x, gmix_ref[...]).astype(BF16)

    val = _dot(h, wc_ref[:, :CONV_CH])
    gate = _dot(h, wc_ref[:, CONV_CH:])
    gsh_ref[0, CONV_HALO:CONV_HALO + tile, :] = val * _sigmoid(gate)
    n_rows = tile + CONV_HALO
    glu = gsh_ref[0].reshape(n_rows // SUBLANES, SUBLANES, CONV_CH)
    sub = lax.broadcasted_iota(jnp.int32, (n_rows // SUBLANES - 1, SUBLANES, CONV_CH), 1)
    for r in range(1, SUBLANES):
        merged = jnp.where(sub < r, glu[1:], glu[:-1])
        shifted = pltpu.roll(merged, SUBLANES - r, axis=1)
        gsh_ref[r, 0:n_rows - SUBLANES, :] = shifted.reshape(n_rows - SUBLANES, CONV_CH)
    lng, lnb, cb = lng_ref[...], lnb_ref[...], cb_ref[...]

    def conv_rows(n0):
        groups = CONV_ROWS // SUBLANES
        y = jnp.broadcast_to(cb, (groups, SUBLANES, CONV_CH))
        for j in range(CONV_K):
            a, r = divmod(j + CONV_HALO - (CONV_K - 1), SUBLANES)
            win = gsh_ref[r, n0 + SUBLANES * a:n0 + SUBLANES * a + CONV_ROWS, :]
            y = y + cw_ref[j] * win.reshape(groups, SUBLANES, CONV_CH)
        y = y.reshape(CONV_ROWS, CONV_CH)
        yc = y - jnp.mean(y, axis=-1, keepdims=True)
        yn = yc * lax.rsqrt(jnp.mean(yc * yc, axis=-1, keepdims=True) + EPS) * lng + lnb
        mixed_ref[n0:n0 + CONV_ROWS, ATTN_WIDTH:] = (yn * _sigmoid(yn)).astype(BF16)

    qn = _head_pair_rms(_dot(h, wq_ref[...]), gq_ref[...]).astype(BF16)
    kv = _dot(h, wkv_ref[...])
    kn = _head_pair_rms(kv[:, :KV_WIDTH], gk_ref[...])
    v = kv[:, KV_WIDTH:]

    lane = lax.broadcasted_iota(jnp.int32, (2 * BLOCK, KV_WIDTH), 1)
    key = lax.broadcasted_iota(jnp.int32, (2 * BLOCK, KV_WIDTH), 0)
    head0 = (lane < HEAD_DIM) & (key > 0)
    head1 = (lane >= HEAD_DIM) & (key > 0)
    ind0 = jnp.where(lane < HEAD_DIM, 1.0, 0.0).astype(BF16)
    ind1 = jnp.where(lane < HEAD_DIM, 0.0, 1.0).astype(BF16)

    def masked(a, keep):
        return jnp.where(keep, a, 0.0).astype(BF16)

    for i in range(n_blocks):
        rows = slice(i * BLOCK, (i + 1) * BLOCK)
        if i == 0:
            k_prev, v_prev = kprev_ref[...], vprev_ref[...]
            bias = bias_ref[jnp.where(step == 0, 1, 0)]
        else:
            prev = slice((i - 1) * BLOCK, i * BLOCK)
            k_prev, v_prev = kn[prev], v[prev]
            bias = bias_ref[0]
        kk = jnp.concatenate([k_prev, kn[rows]], axis=0)
        vv = jnp.concatenate([v_prev, v[rows]], axis=0)
        k_st = jnp.concatenate([masked(kk, head0), masked(kk, head1)], axis=0)
        v_st = jnp.concatenate([jnp.concatenate([masked(vv, head0), ind0], axis=1),
                                jnp.concatenate([masked(vv, head1), ind1], axis=1)], axis=0)
        q_st = jnp.concatenate([qn[rows, j * KV_WIDTH:(j + 1) * KV_WIDTH] for j in range(Q_PER_KV)], axis=0)
        s = _dot_nt(q_st, k_st) + bias
        probs = []
        for c in range(N_KV_HEADS):
            sc = s[:, c * 2 * BLOCK:(c + 1) * 2 * BLOCK]
            probs.append(jnp.exp2(sc - jnp.max(sc, axis=-1, keepdims=True)).astype(BF16))
        o = _dot(jnp.concatenate(probs, axis=1), v_st)
        o = o[:, :KV_WIDTH] / o[:, KV_WIDTH:]
        for j in range(Q_PER_KV):
            mixed_ref[rows, j * KV_WIDTH:(j + 1) * KV_WIDTH] = o[j * BLOCK:(j + 1) * BLOCK].astype(BF16)
        for n0 in range(i * BLOCK, (i + 1) * BLOCK, CONV_ROWS):
            conv_rows(n0)
    kprev_ref[...] = kn[tile - BLOCK:]
    vprev_ref[...] = v[tile - BLOCK:]
    gsh_ref[0, 0:CONV_HALO, :] = gsh_ref[0, tile:tile + CONV_HALO, :]

    o_ref[0] = x + _dot(mixed_ref[...], wout_ref[...])


def _const_spec(shape):
    return pl.BlockSpec(shape, lambda *_: (0,) * len(shape))


def _mixer(x, p):
    batch, seq, _ = x.shape
    tile = MIX_TILE
    tok_spec = pl.BlockSpec((1, tile, D_MODEL), lambda b, s: (b, s, 0))
    consts = [p["gmix"], p["wq"], p["wkv"], p["wc"], p["gq"], p["gk"], p["bias"],
              p["cw"], p["cb"], p["lng"], p["lnb"], p["wout"]]
    return pl.pallas_call(
        _mixer_kernel,
        grid=(batch, seq // tile),
        in_specs=[tok_spec] + [_const_spec(c.shape) for c in consts],
        out_specs=tok_spec,
        out_shape=jax.ShapeDtypeStruct(x.shape, F32),
        scratch_shapes=[
            pltpu.VMEM((BLOCK, KV_WIDTH), F32),
            pltpu.VMEM((BLOCK, KV_WIDTH), F32),
            pltpu.VMEM((SUBLANES, tile + CONV_HALO, CONV_CH), F32),
            pltpu.VMEM((tile, D_MODEL), BF16),
        ],
        compiler_params=pltpu.CompilerParams(
            dimension_semantics=("arbitrary", "arbitrary"), vmem_limit_bytes=VMEM_LIMIT_BYTES),
        name="mixer",
    )(x, *consts)


def _memkv_kernel(mem_ref, gmem_ref, wkv_ref, gk_ref, kt_ref, v_ref):
    mem_n = _rms(mem_ref[0], gmem_ref[...]).astype(BF16)
    kv = _dot(mem_n, wkv_ref[...])
    for hd in range(N_X_HEADS):
        cols = slice(hd * X_HEAD_DIM, (hd + 1) * X_HEAD_DIM)
        kt_ref[0, cols, :] = _rms(kv[:, cols], gk_ref[...]).T.astype(BF16)
    v_ref[0] = kv[:, D_MODEL:].astype(BF16)


def _memkv(mem, p):
    batch = mem.shape[0]
    consts = [p["gmem"], p["wkv_x"], p["gxk"]]
    return pl.pallas_call(
        _memkv_kernel,
        grid=(batch,),
        in_specs=[pl.BlockSpec((1, N_MEM, D_MODEL), lambda b: (b, 0, 0))] + [_const_spec(c.shape) for c in consts],
        out_specs=[pl.BlockSpec((1, D_MODEL, N_MEM), lambda b: (b, 0, 0)),
                   pl.BlockSpec((1, N_MEM, D_MODEL), lambda b: (b, 0, 0))],
        out_shape=[jax.ShapeDtypeStruct((batch, D_MODEL, N_MEM), BF16),
                   jax.ShapeDtypeStruct((batch, N_MEM, D_MODEL), BF16)],
        compiler_params=pltpu.CompilerParams(
            dimension_semantics=("arbitrary",), vmem_limit_bytes=VMEM_LIMIT_BYTES),
        name="memkv",
    )(mem, *consts)


def _xattn_kernel(x_ref, gx_ref, wq_ref, gq_ref, kt_ref, v_ref, wo_ref, o_ref, att_ref):
    x = x_ref[0]
    h = _rms(x, gx_ref[...]).astype(BF16)
    q = _dot(h, wq_ref[...])
    for hd in range(N_X_HEADS):
        cols = slice(hd * X_HEAD_DIM, (hd + 1) * X_HEAD_DIM)
        qn = _rms(q[:, cols], gq_ref[...]).astype(BF16)
        s = _dot(qn, kt_ref[0, cols, :]) * (X_HEAD_DIM ** -0.5)
        p = jnp.exp(s - jnp.max(s, axis=-1, keepdims=True))
        inv = 1.0 / jnp.sum(p, axis=-1, keepdims=True)
        att_ref[:, cols] = (_dot(p.astype(BF16), v_ref[0, :, cols]) * inv).astype(BF16)
    o_ref[0] = x + _dot(att_ref[...], wo_ref[...])


def _xattn(x, kt, v, p):
    batch, seq, _ = x.shape
    tile = XATTN_TILE
    tok_spec = pl.BlockSpec((1, tile, D_MODEL), lambda b, s: (b, s, 0))
    return pl.pallas_call(
        _xattn_kernel,
        grid=(batch, seq // tile),
        in_specs=[tok_spec, _const_spec(p["gx"].shape), _const_spec(p["wq_x"].shape), _const_spec(p["gxq"].shape),
                  pl.BlockSpec((1, D_MODEL, N_MEM), lambda b, s: (b, 0, 0)),
                  pl.BlockSpec((1, N_MEM, D_MODEL), lambda b, s: (b, 0, 0)),
                  _const_spec(p["wo_x"].shape)],
        out_specs=tok_spec,
        out_shape=jax.ShapeDtypeStruct(x.shape, F32),
        scratch_shapes=[pltpu.VMEM((tile, D_MODEL), BF16)],
        compiler_params=pltpu.CompilerParams(
            dimension_semantics=("arbitrary", "arbitrary"), vmem_limit_bytes=VMEM_LIMIT_BYTES),
        name="xattn",
    )(x, p["gx"], p["wq_x"], p["gxq"], kt, v, p["wo_x"])


def _ffn_kernel(x_ref, g_ref, wg_ref, wu_ref, wd_ref, o_ref):
    x = x_ref[...]
    h = _rms(x, g_ref[...]).astype(BF16)
    gate = _dot(h, wg_ref[...])
    up = _dot(h, wu_ref[...])
    act = (gate * _sigmoid(gate) * up).astype(BF16)
    o_ref[...] = x + _dot(act, wd_ref[...])


def _ffn(x, p):
    batch, seq, _ = x.shape
    tile = FFN_TILE
    xf = x.reshape(batch * seq, D_MODEL)
    tok_spec = pl.BlockSpec((tile, D_MODEL), lambda t: (t, 0))
    single = dict(pipeline_mode=pl.Buffered(1))
    out = pl.pallas_call(
        _ffn_kernel,
        grid=(batch * seq // tile,),
        in_specs=[tok_spec, _const_spec(p["gffn"].shape),
                  pl.BlockSpec(p["wg"].shape, lambda t: (0, 0), **single),
                  pl.BlockSpec(p["wu"].shape, lambda t: (0, 0), **single),
                  pl.BlockSpec(p["wd"].shape, lambda t: (0, 0), **single)],
        out_specs=tok_spec,
        out_shape=jax.ShapeDtypeStruct(xf.shape, F32),
        compiler_params=pltpu.CompilerParams(
            dimension_semantics=("arbitrary",), vmem_limit_bytes=VMEM_LIMIT_BYTES),
        name="ffn",
    )(xf, p["gffn"], p["wg"], p["wu"], p["wd"])
    return out.reshape(x.shape)


def _q_head_order():
    cols = []
    for j in range(Q_PER_KV):
        for c in range(N_KV_HEADS):
            hd = j + Q_PER_KV * c
            cols.extend(range(hd * HEAD_DIM, (hd + 1) * HEAD_DIM))
    return np.asarray(cols, dtype=np.int32)


def _swa_bias(sinks):
    qi = np.arange(BLOCK)[:, None]
    sk = np.arange(2 * BLOCK)[None, :]
    dist = qi + BLOCK - sk
    valid = (dist >= 0) & (dist < BLOCK)
    slopes = 2.0 ** (-8.0 * (np.arange(N_Q_HEADS, dtype=np.float32) + 1.0) / N_Q_HEADS)
    table = np.empty((2, Q_PER_KV * BLOCK, N_KV_HEADS * 2 * BLOCK), np.float32)
    for first in range(2):
        ok = valid & (sk >= BLOCK) if first else valid
        for j in range(Q_PER_KV):
            for c in range(N_KV_HEADS):
                slope = slopes[j + Q_PER_KV * c]
                blk = np.where(ok, -slope * LOG2E * dist.astype(np.float32), np.float32(NEG))
                table[first, j * BLOCK:(j + 1) * BLOCK, c * 2 * BLOCK:(c + 1) * 2 * BLOCK] = blk
    assert not valid[:, 0].any()
    head_of_row = np.repeat(np.arange(Q_PER_KV), BLOCK)
    table = jnp.asarray(table)
    for c in range(N_KV_HEADS):
        logit = sinks[head_of_row + Q_PER_KV * c].astype(F32) * LOG2E
        table = table.at[:, :, c * 2 * BLOCK].set(jnp.broadcast_to(logit, (2, Q_PER_KV * BLOCK)))
    return table


def _layer_params(l, norm_mix_g, w_in, q_norm_g, k_norm_g, sinks, conv_w, conv_b, conv_ln_g, conv_ln_b,
                  w_out, norm_x_g, norm_mem_g, wq_x, wkv_x, xq_norm_g, xk_norm_g, wo_x, norm_ffn_g,
                  w_gate_up, w_down):
    order = _q_head_order()
    row = lambda a: a.reshape(1, -1).astype(F32)
    w_in_l = w_in[l]
    w_out_l = w_out[l]
    return dict(
        gmix=row(norm_mix_g[l]),
        wq=w_in_l[:, :ATTN_WIDTH][:, order].astype(BF16),
        wkv=w_in_l[:, ATTN_WIDTH:ATTN_WIDTH + 2 * KV_WIDTH].astype(BF16),
        wc=w_in_l[:, ATTN_WIDTH + 2 * KV_WIDTH:].astype(BF16),
        gq=row(jnp.tile(q_norm_g[l], N_Q_HEADS)) * (HEAD_DIM ** -0.5 * LOG2E),
        gk=row(jnp.tile(k_norm_g[l], N_KV_HEADS)),
        bias=_swa_bias(sinks[l]),
        cw=jnp.broadcast_to(conv_w[l].astype(F32)[:, None, :], (CONV_K, SUBLANES, CONV_CH)),
        cb=row(conv_b[l]),
        lng=row(conv_ln_g[l]),
        lnb=row(conv_ln_b[l]),
        wout=jnp.concatenate([w_out_l[:ATTN_WIDTH][order], w_out_l[ATTN_WIDTH:]], axis=0).astype(BF16),
        gx=row(norm_x_g[l]),
        gmem=row(norm_mem_g[l]),
        wq_x=wq_x[l].astype(BF16),
        wkv_x=wkv_x[l].astype(BF16),
        gxq=row(xq_norm_g[l]),
        gxk=row(xk_norm_g[l]),
        wo_x=wo_x[l].astype(BF16),
        gffn=row(norm_ffn_g[l]),
        wg=w_gate_up[l][:, :D_FF].astype(BF16),
        wu=w_gate_up[l][:, D_FF:].astype(BF16),
        wd=w_down[l].astype(BF16),
    )


def kernel(x, mem, norm_mix_g, w_in, q_norm_g, k_norm_g, sinks, conv_w, conv_b, conv_ln_g, conv_ln_b, w_out, norm_x_g, norm_mem_g, wq_x, wkv_x, xq_norm_g, xk_norm_g, wo_x, norm_ffn_g, w_gate_up, w_down):
    depth = w_in.shape[0]
    for l in range(depth):
        p = _layer_params(l, norm_mix_g, w_in, q_norm_g, k_norm_g, sinks, conv_w, conv_b, conv_ln_g,
                          conv_ln_b, w_out, norm_x_g, norm_mem_g, wq_x, wkv_x, xq_norm_g, xk_norm_g,
                          wo_x, norm_ffn_g, w_gate_up, w_down)
        x = _mixer(x, p)
        kt, v = _memkv(mem, p)
        x = _xattn(x, kt, v, p)
        x = _ffn(x, p)
    return x
```

```python
import functools

import jax
import jax.numpy as jnp
import numpy as np
from jax import lax
from jax.experimental import pallas as pl
from jax.experimental.pallas import tpu as pltpu

D_MODEL = 1024
N_MEM = 256
HEAD_DIM = 64
N_Q_HEADS = 8
N_KV_HEADS = 2
Q_PER_KV = N_Q_HEADS // N_KV_HEADS
ATTN_WIDTH = N_Q_HEADS * HEAD_DIM
KV_WIDTH = N_KV_HEADS * HEAD_DIM
CONV_CH = D_MODEL - ATTN_WIDTH
CONV_K = 31
BLOCK = 128
N_X_HEADS = 4
X_HEAD_DIM = D_MODEL // N_X_HEADS
D_FF = 2816
EPS = 1e-6
NEG = -1e30

SUBLANES = 8
CONV_HALO = 32
CONV_ROWS = 32
LOG2E = 1.4426950408889634
TILE = 256
PIPELINE_LAG = 2
VMEM_LIMIT_BYTES = 60 * 1024 * 1024

BF16 = jnp.bfloat16
F32 = jnp.float32


def _dot(a, b):
    return jnp.dot(a, b, preferred_element_type=F32)


def _dot_nt(a, b):
    return lax.dot_general(a, b, (((1,), (1,)), ((), ())), preferred_element_type=F32)


def _rms(x, g):
    return x * lax.rsqrt(jnp.mean(x * x, axis=-1, keepdims=True) + EPS) * g


def _sigmoid(x):
    return 1.0 / (1.0 + jnp.exp(-x))


def _head_pair_rms(x, gain):
    lane = lax.broadcasted_iota(jnp.int32, (x.shape[0], 2 * HEAD_DIM), 1)
    first = lane < HEAD_DIM
    out = []
    for j in range(x.shape[1] // (2 * HEAD_DIM)):
        cols = slice(j * 2 * HEAD_DIM, (j + 1) * 2 * HEAD_DIM)
        blk = x[:, cols]
        sq = blk * blk
        ss_a = jnp.sum(jnp.where(first, sq, 0.0), axis=-1, keepdims=True)
        ss_b = jnp.sum(jnp.where(first, 0.0, sq), axis=-1, keepdims=True)
        inv = jnp.where(first, lax.rsqrt(ss_a * (1.0 / HEAD_DIM) + EPS), lax.rsqrt(ss_b * (1.0 / HEAD_DIM) + EPS))
        out.append(blk * inv * gain[:, cols])
    return out[0] if len(out) == 1 else jnp.concatenate(out, axis=1)


def _mixer_front(x, gmix_ref, wq_ref, wkv_ref, wc_ref, gq_ref, gk_ref, cw_ref, cb_ref, lng_ref, lnb_ref,
                 gsh_ref, mixed_ref):
    tile = x.shape[0]
    h = _rms(x, gmix_ref[...]).astype(BF16)

    val = _dot(h, wc_ref[:, :CONV_CH])
    gate = _dot(h, wc_ref[:, CONV_CH:])
    gsh_ref[0, CONV_HALO:CONV_HALO + tile, :] = val * _sigmoid(gate)
    n_rows = tile + CONV_HALO
    glu = gsh_ref[0].reshape(n_rows // SUBLANES, SUBLANES, CONV_CH)
    sub = lax.broadcasted_iota(jnp.int32, (n_rows // SUBLANES - 1, SUBLANES, CONV_CH), 1)
    for r in range(1, SUBLANES):
        merged = jnp.where(sub < r, glu[1:], glu[:-1])
        shifted = pltpu.roll(merged, SUBLANES - r, axis=1)
        gsh_ref[r, 0:n_rows - SUBLANES, :] = shifted.reshape(n_rows - SUBLANES, CONV_CH)
    lng, lnb, cb = lng_ref[...], lnb_ref[...], cb_ref[...]
    groups = CONV_ROWS // SUBLANES
    for n0 in range(0, tile, CONV_ROWS):
        y = jnp.broadcast_to(cb, (groups, SUBLANES, CONV_CH))
        for j in range(CONV_K):
            a, r = divmod(j + CONV_HALO - (CONV_K - 1), SUBLANES)
            win = gsh_ref[r, n0 + SUBLANES * a:n0 + SUBLANES * a + CONV_ROWS, :]
            y = y + cw_ref[j] * win.reshape(groups, SUBLANES, CONV_CH)
        y = y.reshape(CONV_ROWS, CONV_CH)
        yc = y - jnp.mean(y, axis=-1, keepdims=True)
        yn = yc * lax.rsqrt(jnp.mean(yc * yc, axis=-1, keepdims=True) + EPS) * lng + lnb
        mixed_ref[n0:n0 + CONV_ROWS, ATTN_WIDTH:] = (yn * _sigmoid(yn)).astype(BF16)

    gsh_ref[0, 0:CONV_HALO, :] = gsh_ref[0, tile:tile + CONV_HALO, :]

    qn = _head_pair_rms(_dot(h, wq_ref[...]), gq_ref[...]).astype(BF16)
    kv = _dot(h, wkv_ref[...])
    kn = _head_pair_rms(kv[:, :KV_WIDTH], gk_ref[...])
    return qn, kn, kv[:, KV_WIDTH:]


def _mixer_back(x, qn, kn, v, first_of_seq, bias_ref, wout_ref, kprev_ref, vprev_ref, mixed_ref):
    tile = x.shape[0]
    n_blocks = tile // BLOCK

    lane = lax.broadcasted_iota(jnp.int32, (2 * BLOCK, KV_WIDTH), 1)
    key = lax.broadcasted_iota(jnp.int32, (2 * BLOCK, KV_WIDTH), 0)
    head0 = (lane < HEAD_DIM) & (key > 0)
    head1 = (lane >= HEAD_DIM) & (key > 0)
    ind0 = jnp.where(lane < HEAD_DIM, 1.0, 0.0).astype(BF16)
    ind1 = jnp.where(lane < HEAD_DIM, 0.0, 1.0).astype(BF16)

    def masked(a, keep):
        return jnp.where(keep, a, 0.0).astype(BF16)

    for i in range(n_blocks):
        rows = slice(i * BLOCK, (i + 1) * BLOCK)
        if i == 0:
            k_prev, v_prev = kprev_ref[...], vprev_ref[...]
            bias = bias_ref[jnp.where(first_of_seq, 1, 0)]
        else:
            prev = slice((i - 1) * BLOCK, i * BLOCK)
            k_prev, v_prev = kn[prev], v[prev]
            bias = bias_ref[0]
        kk = jnp.concatenate([k_prev, kn[rows]], axis=0)
        vv = jnp.concatenate([v_prev, v[rows]], axis=0)
        k_st = jnp.concatenate([masked(kk, head0), masked(kk, head1)], axis=0)
        v_st = jnp.concatenate([jnp.concatenate([masked(vv, head0), ind0], axis=1),
                                jnp.concatenate([masked(vv, head1), ind1], axis=1)], axis=0)
        q_st = jnp.concatenate([qn[rows, j * KV_WIDTH:(j + 1) * KV_WIDTH] for j in range(Q_PER_KV)], axis=0)
        s = _dot_nt(q_st, k_st) + bias
        probs = []
        for c in range(N_KV_HEADS):
            sc = s[:, c * 2 * BLOCK:(c + 1) * 2 * BLOCK]
            probs.append(jnp.exp2(sc - jnp.max(sc, axis=-1, keepdims=True)).astype(BF16))
        o = _dot(jnp.concatenate(probs, axis=1), v_st)
        o = o[:, :KV_WIDTH] / o[:, KV_WIDTH:]
        for j in range(Q_PER_KV):
            mixed_ref[rows, j * KV_WIDTH:(j + 1) * KV_WIDTH] = o[j * BLOCK:(j + 1) * BLOCK].astype(BF16)
    kprev_ref[...] = kn[tile - BLOCK:]
    vprev_ref[...] = v[tile - BLOCK:]

    return x + _dot(mixed_ref[...], wout_ref[...])


def _xattn_stage(x, gx_ref, wq_ref, gq_ref, kt_ref, v_ref, wo_ref, att_ref):
    h = _rms(x, gx_ref[...]).astype(BF16)
    q = _dot(h, wq_ref[...])
    for hd in range(N_X_HEADS):
        cols = slice(hd * X_HEAD_DIM, (hd + 1) * X_HEAD_DIM)
        qn = _rms(q[:, cols], gq_ref[...]).astype(BF16)
        s = _dot(qn, kt_ref[0, cols, :])
        p = jnp.exp2(s - jnp.max(s, axis=-1, keepdims=True))
        inv = 1.0 / jnp.sum(p, axis=-1, keepdims=True)
        att_ref[:, cols] = (_dot(p.astype(BF16), v_ref[0, :, cols]) * inv).astype(BF16)
    return x + _dot(att_ref[...], wo_ref[...])


def _ffn_stage(x, g_ref, wg_ref, wu_ref, wd_ref):
    h = _rms(x, g_ref[...]).astype(BF16)
    gate = _dot(h, wg_ref[...])
    up = _dot(h, wu_ref[...])
    act = (gate * _sigmoid(gate) * up).astype(BF16)
    return x + _dot(act, wd_ref[...])


def _layer_kernel(tiles_per_seq,
                  x_ref, kt_ref, v_ref,
                  gmix_ref, wq_ref, wkv_ref, wc_ref, gq_ref, gk_ref, bias_ref, cw_ref, cb_ref, lng_ref, lnb_ref,
                  wout_ref, gx_ref, wqx_ref, gxq_ref, wox_ref, gffn_ref, wg_ref, wu_ref, wd_ref,
                  o_ref,
                  kprev_ref, vprev_ref, gsh_ref, mixed_ref, att_ref, x1_ref, x2_ref):
    step = pl.program_id(0)
    first_of_seq = step % tiles_per_seq == 0

    @pl.when(step == 0)
    def _():
        x1_ref[...] = jnp.zeros_like(x1_ref)
        x2_ref[...] = jnp.zeros_like(x2_ref)

    @pl.when(first_of_seq)
    def _():
        kprev_ref[...] = jnp.zeros_like(kprev_ref)
        vprev_ref[...] = jnp.zeros_like(vprev_ref)
        gsh_ref[0, 0:CONV_HALO, :] = jnp.zeros((CONV_HALO, CONV_CH), F32)

    x1_prev = x1_ref[...]
    x2_prev = x2_ref[...]
    x = x_ref[...]
    qn, kn, v = _mixer_front(x, gmix_ref, wq_ref, wkv_ref, wc_ref, gq_ref, gk_ref, cw_ref, cb_ref, lng_ref,
                             lnb_ref, gsh_ref, mixed_ref)
    o_ref[...] = _ffn_stage(x2_prev, gffn_ref, wg_ref, wu_ref, wd_ref)
    x2_ref[...] = _xattn_stage(x1_prev, gx_ref, wqx_ref, gxq_ref, kt_ref, v_ref, wox_ref, att_ref)
    x1_ref[...] = _mixer_back(x, qn, kn, v, first_of_seq, bias_ref, wout_ref, kprev_ref, vprev_ref, mixed_ref)


def _layer(x, kt, v, p):
    batch, seq, _ = x.shape
    tiles_per_seq = seq // TILE
    n_tiles = batch * tiles_per_seq
    xf = x.reshape(batch * seq, D_MODEL)

    def resident(a):
        return pl.BlockSpec(a.shape, lambda i: (0,) * a.ndim, pipeline_mode=pl.Buffered(1))

    def mem_batch(i):
        return jnp.clip(i - 1, 0, n_tiles - 1) // tiles_per_seq

    consts = [p["gmix"], p["wq"], p["wkv"], p["wc"], p["gq"], p["gk"], p["bias"], p["cw"], p["cb"], p["lng"],
              p["lnb"], p["wout"], p["gx"], p["wq_x"], p["gxq"], p["wo_x"], p["gffn"], p["wg"], p["wu"], p["wd"]]
    out = pl.pallas_call(
        functools.partial(_layer_kernel, tiles_per_seq),
        grid=(n_tiles + PIPELINE_LAG,),
        in_specs=[pl.BlockSpec((TILE, D_MODEL), lambda i: (jnp.minimum(i, n_tiles - 1), 0)),
                  pl.BlockSpec((1, D_MODEL, N_MEM), lambda i: (mem_batch(i), 0, 0)),
                  pl.BlockSpec((1, N_MEM, D_MODEL), lambda i: (mem_batch(i), 0, 0))]
                 + [resident(c) for c in consts],
        out_specs=pl.BlockSpec((TILE, D_MODEL), lambda i: (jnp.maximum(i - PIPELINE_LAG, 0), 0)),
        out_shape=jax.ShapeDtypeStruct(xf.shape, F32),
        scratch_shapes=[
            pltpu.VMEM((BLOCK, KV_WIDTH), F32),
            pltpu.VMEM((BLOCK, KV_WIDTH), F32),
            pltpu.VMEM((SUBLANES, TILE + CONV_HALO, CONV_CH), F32),
            pltpu.VMEM((TILE, D_MODEL), BF16),
            pltpu.VMEM((TILE, D_MODEL), BF16),
            pltpu.VMEM((TILE, D_MODEL), F32),
            pltpu.VMEM((TILE, D_MODEL), F32),
        ],
        compiler_params=pltpu.CompilerParams(
            dimension_semantics=("arbitrary",), vmem_limit_bytes=VMEM_LIMIT_BYTES),
        name="layer",
    )(xf, kt, v, *consts)
    return out.reshape(x.shape)


def _memkv_kernel(mem_ref, gmem_ref, wkv_ref, gk_ref, kt_ref, v_ref):
    mem_n = _rms(mem_ref[0], gmem_ref[...]).astype(BF16)
    kv = _dot(mem_n, wkv_ref[...])
    for hd in range(N_X_HEADS):
        cols = slice(hd * X_HEAD_DIM, (hd + 1) * X_HEAD_DIM)
        kt_ref[0, cols, :] = _rms(kv[:, cols], gk_ref[...]).T.astype(BF16)
    v_ref[0] = kv[:, D_MODEL:].astype(BF16)


def _memkv(mem, p):
    batch = mem.shape[0]
    consts = [p["gmem"], p["wkv_x"], p["gxk"]]
    return pl.pallas_call(
        _memkv_kernel,
        grid=(batch,),
        in_specs=[pl.BlockSpec((1, N_MEM, D_MODEL), lambda b: (b, 0, 0))]
                 + [pl.BlockSpec(c.shape, lambda b: (0, 0)) for c in consts],
        out_specs=[pl.BlockSpec((1, D_MODEL, N_MEM), lambda b: (b, 0, 0)),
                   pl.BlockSpec((1, N_MEM, D_MODEL), lambda b: (b, 0, 0))],
        out_shape=[jax.ShapeDtypeStruct((batch, D_MODEL, N_MEM), BF16),
                   jax.ShapeDtypeStruct((batch, N_MEM, D_MODEL), BF16)],
        compiler_params=pltpu.CompilerParams(
            dimension_semantics=("arbitrary",), vmem_limit_bytes=VMEM_LIMIT_BYTES),
        name="memkv",
    )(mem, *consts)


def _q_head_order():
    cols = []
    for j in range(Q_PER_KV):
        for c in range(N_KV_HEADS):
            hd = j + Q_PER_KV * c
            cols.extend(range(hd * HEAD_DIM, (hd + 1) * HEAD_DIM))
    return np.asarray(cols, dtype=np.int32)


def _swa_bias(sinks):
    qi = np.arange(BLOCK)[:, None]
    sk = np.arange(2 * BLOCK)[None, :]
    dist = qi + BLOCK - sk
    valid = (dist >= 0) & (dist < BLOCK)
    slopes = 2.0 ** (-8.0 * (np.arange(N_Q_HEADS, dtype=np.float32) + 1.0) / N_Q_HEADS)
    table = np.empty((2, Q_PER_KV * BLOCK, N_KV_HEADS * 2 * BLOCK), np.float32)
    for first in range(2):
        ok = valid & (sk >= BLOCK) if first else valid
        for j in range(Q_PER_KV):
            for c in range(N_KV_HEADS):
                slope = slopes[j + Q_PER_KV * c]
                blk = np.where(ok, -slope * LOG2E * dist.astype(np.float32), np.float32(NEG))
                table[first, j * BLOCK:(j + 1) * BLOCK, c * 2 * BLOCK:(c + 1) * 2 * BLOCK] = blk
    assert not valid[:, 0].any()
    head_of_row = np.repeat(np.arange(Q_PER_KV), BLOCK)
    table = jnp.asarray(table)
    for c in range(N_KV_HEADS):
        logit = sinks[head_of_row + Q_PER_KV * c].astype(F32) * LOG2E
        table = table.at[:, :, c * 2 * BLOCK].set(jnp.broadcast_to(logit, (2, Q_PER_KV * BLOCK)))
    return table


def _layer_params(l, norm_mix_g, w_in, q_norm_g, k_norm_g, sinks, conv_w, conv_b, conv_ln_g, conv_ln_b,
                  w_out, norm_x_g, norm_mem_g, wq_x, wkv_x, xq_norm_g, xk_norm_g, wo_x, norm_ffn_g,
                  w_gate_up, w_down):
    order = _q_head_order()
    row = lambda a: a.reshape(1, -1).astype(F32)
    w_in_l = w_in[l]
    w_out_l = w_out[l]
    return dict(
        gmix=row(norm_mix_g[l]),
        wq=w_in_l[:, :ATTN_WIDTH][:, order].astype(BF16),
        wkv=w_in_l[:, ATTN_WIDTH:ATTN_WIDTH + 2 * KV_WIDTH].astype(BF16),
        wc=w_in_l[:, ATTN_WIDTH + 2 * KV_WIDTH:].astype(BF16),
        gq=row(jnp.tile(q_norm_g[l], N_Q_HEADS)) * (HEAD_DIM ** -0.5 * LOG2E),
        gk=row(jnp.tile(k_norm_g[l], N_KV_HEADS)),
        bias=_swa_bias(sinks[l]),
        cw=jnp.broadcast_to(conv_w[l].astype(F32)[:, None, :], (CONV_K, SUBLANES, CONV_CH)),
        cb=row(conv_b[l]),
        lng=row(conv_ln_g[l]),
        lnb=row(conv_ln_b[l]),
        wout=jnp.concatenate([w_out_l[:ATTN_WIDTH][order], w_out_l[ATTN_WIDTH:]], axis=0).astype(BF16),
        gx=row(norm_x_g[l]),
        gmem=row(norm_mem_g[l]),
        wq_x=wq_x[l].astype(BF16),
        wkv_x=wkv_x[l].astype(BF16),
        gxq=row(xq_norm_g[l]) * (X_HEAD_DIM ** -0.5 * LOG2E),
        gxk=row(xk_norm_g[l]),
        wo_x=wo_x[l].astype(BF16),
        gffn=row(norm_ffn_g[l]),
        wg=w_gate_up[l][:, :D_FF].astype(BF16),
        wu=w_gate_up[l][:, D_FF:].astype(BF16),
        wd=w_down[l].astype(BF16),
    )


def kernel(x, mem, norm_mix_g, w_in, q_norm_g, k_norm_g, sinks, conv_w, conv_b, conv_ln_g, conv_ln_b, w_out, norm_x_g, norm_mem_g, wq_x, wkv_x, xq_norm_g, xk_norm_g, wo_x, norm_ffn_g, w_gate_up, w_down):
    depth = w_in.shape[0]
    for l in range(depth):
        p = _layer_params(l, norm_mix_g, w_in, q_norm_g, k_norm_g, sinks, conv_w, conv_b, conv_ln_g,
                          conv_ln_b, w_out, norm_x_g, norm_mem_g, wq_x, wkv_x, xq_norm_g, xk_norm_g,
                          wo_x, norm_ffn_g, w_gate_up, w_down)
        kt, v = _memkv(mem, p)
        x = _layer(x, kt, v, p)
    return x
```

```python
import functools

import jax
import jax.numpy as jnp
import numpy as np
from jax import lax
from jax.experimental import pallas as pl
from jax.experimental.pallas import tpu as pltpu

D_MODEL = 1024
N_MEM = 256
HEAD_DIM = 64
N_Q_HEADS = 8
N_KV_HEADS = 2
Q_PER_KV = N_Q_HEADS // N_KV_HEADS
ATTN_WIDTH = N_Q_HEADS * HEAD_DIM
KV_WIDTH = N_KV_HEADS * HEAD_DIM
CONV_CH = D_MODEL - ATTN_WIDTH
CONV_K = 31
BLOCK = 128
N_X_HEADS = 4
X_HEAD_DIM = D_MODEL // N_X_HEADS
D_FF = 2816
EPS = 1e-6
NEG = -1e30

SUBLANES = 8
CONV_HALO = 32
CONV_ROWS = 32
LOG2E = 1.4426950408889634
TILE = 256
PIPELINE_LAG = 2
FFN_CHUNKS = ((0, 768), (768, 1536), (1536, 2304), (2304, D_FF))
VMEM_LIMIT_BYTES = 60 * 1024 * 1024

BF16 = jnp.bfloat16
F32 = jnp.float32


def _dot(a, b):
    return jnp.dot(a, b, preferred_element_type=F32)


def _dot_nt(a, b):
    return lax.dot_general(a, b, (((1,), (1,)), ((), ())), preferred_element_type=F32)


def _rms(x, g):
    return x * lax.rsqrt(jnp.mean(x * x, axis=-1, keepdims=True) + EPS) * g


def _sigmoid(x):
    return 1.0 / (1.0 + jnp.exp(-x))


def _head_pair_rms(x, gain):
    lane = lax.broadcasted_iota(jnp.int32, (x.shape[0], 2 * HEAD_DIM), 1)
    first = lane < HEAD_DIM
    out = []
    for j in range(x.shape[1] // (2 * HEAD_DIM)):
        cols = slice(j * 2 * HEAD_DIM, (j + 1) * 2 * HEAD_DIM)
        blk = x[:, cols]
        sq = blk * blk
        ss_a = jnp.sum(jnp.where(first, sq, 0.0), axis=-1, keepdims=True)
        ss_b = jnp.sum(jnp.where(first, 0.0, sq), axis=-1, keepdims=True)
        inv = jnp.where(first, lax.rsqrt(ss_a * (1.0 / HEAD_DIM) + EPS), lax.rsqrt(ss_b * (1.0 / HEAD_DIM) + EPS))
        out.append(blk * inv * gain[:, cols])
    return out[0] if len(out) == 1 else jnp.concatenate(out, axis=1)


def _conv_prepare(h, wc_ref, gsh_ref):
    tile = h.shape[0]
    val = _dot(h, wc_ref[:, :CONV_CH])
    gate = _dot(h, wc_ref[:, CONV_CH:])
    gsh_ref[0, CONV_HALO:CONV_HALO + tile, :] = val * _sigmoid(gate)
    n_rows = tile + CONV_HALO
    glu = gsh_ref[0].reshape(n_rows // SUBLANES, SUBLANES, CONV_CH)
    sub = lax.broadcasted_iota(jnp.int32, (n_rows // SUBLANES - 1, SUBLANES, CONV_CH), 1)
    for r in range(1, SUBLANES):
        merged = jnp.where(sub < r, glu[1:], glu[:-1])
        shifted = pltpu.roll(merged, SUBLANES - r, axis=1)
        gsh_ref[r, 0:n_rows - SUBLANES, :] = shifted.reshape(n_rows - SUBLANES, CONV_CH)


def _conv_rows(n0, gsh_ref, cw_ref, cb_ref, lng_ref, lnb_ref, mixed_ref):
    groups = CONV_ROWS // SUBLANES
    y = jnp.broadcast_to(cb_ref[...], (groups, SUBLANES, CONV_CH))
    for j in range(CONV_K):
        a, r = divmod(j + CONV_HALO - (CONV_K - 1), SUBLANES)
        win = gsh_ref[r, n0 + SUBLANES * a:n0 + SUBLANES * a + CONV_ROWS, :]
        y = y + cw_ref[j] * win.reshape(groups, SUBLANES, CONV_CH)
    y = y.reshape(CONV_ROWS, CONV_CH)
    yc = y - jnp.mean(y, axis=-1, keepdims=True)
    yn = yc * lax.rsqrt(jnp.mean(yc * yc, axis=-1, keepdims=True) + EPS) * lng_ref[...] + lnb_ref[...]
    mixed_ref[n0:n0 + CONV_ROWS, ATTN_WIDTH:] = (yn * _sigmoid(yn)).astype(BF16)


def _swa_probs(i, qn, kn, v, first_of_seq, bias_ref, kprev_ref, vprev_ref):
    lane = lax.broadcasted_iota(jnp.int32, (2 * BLOCK, KV_WIDTH), 1)
    key = lax.broadcasted_iota(jnp.int32, (2 * BLOCK, KV_WIDTH), 0)
    head0 = (lane < HEAD_DIM) & (key > 0)
    head1 = (lane >= HEAD_DIM) & (key > 0)
    ind0 = jnp.where(lane < HEAD_DIM, 1.0, 0.0).astype(BF16)
    ind1 = jnp.where(lane < HEAD_DIM, 0.0, 1.0).astype(BF16)

    def masked(a, keep):
        return jnp.where(keep, a, 0.0).astype(BF16)

    rows = slice(i * BLOCK, (i + 1) * BLOCK)
    if i == 0:
        k_prev, v_prev = kprev_ref[...], vprev_ref[...]
        bias = bias_ref[jnp.where(first_of_seq, 1, 0)]
    else:
        prev = slice((i - 1) * BLOCK, i * BLOCK)
        k_prev, v_prev = kn[prev], v[prev]
        bias = bias_ref[0]
    kk = jnp.concatenate([k_prev, kn[rows]], axis=0)
    vv = jnp.concatenate([v_prev, v[rows]], axis=0)
    k_st = jnp.concatenate([masked(kk, head0), masked(kk, head1)], axis=0)
    v_st = jnp.concatenate([jnp.concatenate([masked(vv, head0), ind0], axis=1),
                            jnp.concatenate([masked(vv, head1), ind1], axis=1)], axis=0)
    q_st = jnp.concatenate([qn[rows, j * KV_WIDTH:(j + 1) * KV_WIDTH] for j in range(Q_PER_KV)], axis=0)
    s = _dot_nt(q_st, k_st) + bias
    probs = []
    for c in range(N_KV_HEADS):
        sc = s[:, c * 2 * BLOCK:(c + 1) * 2 * BLOCK]
        probs.append(jnp.exp2(sc - jnp.max(sc, axis=-1, keepdims=True)).astype(BF16))
    return jnp.concatenate(probs, axis=1), v_st


def _swa_out(i, probs, v_st, mixed_ref):
    o = _dot(probs, v_st)
    o = o[:, :KV_WIDTH] / o[:, KV_WIDTH:]
    for j in range(Q_PER_KV):
        mixed_ref[i * BLOCK:(i + 1) * BLOCK, j * KV_WIDTH:(j + 1) * KV_WIDTH] = (
            o[j * BLOCK:(j + 1) * BLOCK].astype(BF16))


def _xattn_probs(hd, q, gq_ref, kt_ref):
    cols = slice(hd * X_HEAD_DIM, (hd + 1) * X_HEAD_DIM)
    qn = _rms(q[:, cols], gq_ref[...]).astype(BF16)
    s = _dot(qn, kt_ref[0, cols, :])
    p = jnp.exp2(s - jnp.max(s, axis=-1, keepdims=True))
    return p.astype(BF16), 1.0 / jnp.sum(p, axis=-1, keepdims=True)


def _xattn_out(hd, p, inv, v_ref, att_ref):
    cols = slice(hd * X_HEAD_DIM, (hd + 1) * X_HEAD_DIM)
    att_ref[:, cols] = (_dot(p, v_ref[0, :, cols]) * inv).astype(BF16)


def _ffn_act(h, c0, c1, wg_ref, wu_ref):
    gate = _dot(h, wg_ref[:, c0:c1])
    up = _dot(h, wu_ref[:, c0:c1])
    return (gate * _sigmoid(gate) * up).astype(BF16)


def _layer_kernel(tiles_per_seq,
                  x_ref, kt_ref, v_ref,
                  gmix_ref, wq_ref, wkv_ref, wc_ref, gq_ref, gk_ref, bias_ref, cw_ref, cb_ref, lng_ref, lnb_ref,
                  wout_ref, gx_ref, wqx_ref, gxq_ref, wox_ref, gffn_ref, wg_ref, wu_ref, wd_ref,
                  o_ref,
                  kprev_ref, vprev_ref, gsh_ref, mixed_ref, att_ref, x1_ref, x2_ref):
    step = pl.program_id(0)
    first_of_seq = step % tiles_per_seq == 0

    @pl.when(step == 0)
    def _():
        x1_ref[...] = jnp.zeros_like(x1_ref)
        x2_ref[...] = jnp.zeros_like(x2_ref)

    @pl.when(first_of_seq)
    def _():
        kprev_ref[...] = jnp.zeros_like(kprev_ref)
        vprev_ref[...] = jnp.zeros_like(vprev_ref)
        gsh_ref[0, 0:CONV_HALO, :] = jnp.zeros((CONV_HALO, CONV_CH), F32)

    conv = functools.partial(_conv_rows, gsh_ref=gsh_ref, cw_ref=cw_ref, cb_ref=cb_ref, lng_ref=lng_ref,
                             lnb_ref=lnb_ref, mixed_ref=mixed_ref)
    conv_starts = list(range(0, TILE, CONV_ROWS))

    def ffn_act(c):
        return _ffn_act(h_ffn, *FFN_CHUNKS[c], wg_ref, wu_ref)

    def ffn_down(c, act):
        c0, c1 = FFN_CHUNKS[c]
        return _dot(act, wd_ref[c0:c1, :])

    h_ffn = _rms(x2_ref[...], gffn_ref[...]).astype(BF16)
    act = ffn_act(0)
    h_mix = _rms(x_ref[...], gmix_ref[...]).astype(BF16)
    _conv_prepare(h_mix, wc_ref, gsh_ref)
    y = ffn_down(0, act)
    act = ffn_act(1)
    for n0 in conv_starts[0:2]:
        conv(n0)
    h_x = _rms(x1_ref[...], gx_ref[...]).astype(BF16)
    qn = _head_pair_rms(_dot(h_mix, wq_ref[...]), gq_ref[...]).astype(BF16)
    kv = _dot(h_mix, wkv_ref[...])
    kn = _head_pair_rms(kv[:, :KV_WIDTH], gk_ref[...])
    v = kv[:, KV_WIDTH:]
    q_x = _dot(h_x, wqx_ref[...])
    swa_probs = functools.partial(_swa_probs, qn=qn, kn=kn, v=v, first_of_seq=first_of_seq, bias_ref=bias_ref,
                                  kprev_ref=kprev_ref, vprev_ref=vprev_ref)
    xattn_probs = functools.partial(_xattn_probs, q=q_x, gq_ref=gxq_ref, kt_ref=kt_ref)
    swa0 = swa_probs(0)
    xp = [xattn_probs(0), xattn_probs(1)]
    for n0 in conv_starts[2:4]:
        conv(n0)
    y = y + ffn_down(1, act)
    act = ffn_act(2)
    _swa_out(0, *swa0, mixed_ref)
    for hd in (0, 1):
        _xattn_out(hd, *xp[hd], v_ref, att_ref)
    swa1 = swa_probs(1)
    xp = [xattn_probs(2), xattn_probs(3)]
    for n0 in conv_starts[4:6]:
        conv(n0)
    y = y + ffn_down(2, act)
    act = ffn_act(3)
    _swa_out(1, *swa1, mixed_ref)
    for hd in (2, 3):
        _xattn_out(hd, *xp[hd - 2], v_ref, att_ref)
    for n0 in conv_starts[6:8]:
        conv(n0)
    gsh_ref[0, 0:CONV_HALO, :] = gsh_ref[0, TILE:TILE + CONV_HALO, :]
    x2_new = x1_ref[...] + _dot(att_ref[...], wox_ref[...])
    y = y + ffn_down(3, act)
    x1_new = x_ref[...] + _dot(mixed_ref[...], wout_ref[...])
    kprev_ref[...] = kn[TILE - BLOCK:]
    vprev_ref[...] = v[TILE - BLOCK:]
    o_ref[...] = x2_ref[...] + y
    x2_ref[...] = x2_new
    x1_ref[...] = x1_new


def _layer(x, kt, v, p):
    batch, seq, _ = x.shape
    tiles_per_seq = seq // TILE
    n_tiles = batch * tiles_per_seq
    xf = x.reshape(batch * seq, D_MODEL)

    def resident(a):
        return pl.BlockSpec(a.shape, lambda i: (0,) * a.ndim, pipeline_mode=pl.Buffered(1))

    def mem_batch(i):
        return jnp.clip(i - 1, 0, n_tiles - 1) // tiles_per_seq

    consts = [p["gmix"], p["wq"], p["wkv"], p["wc"], p["gq"], p["gk"], p["bias"], p["cw"], p["cb"], p["lng"],
              p["lnb"], p["wout"], p["gx"], p["wq_x"], p["gxq"], p["wo_x"], p["gffn"], p["wg"], p["wu"], p["wd"]]
    out = pl.pallas_call(
        functools.partial(_layer_kernel, tiles_per_seq),
        grid=(n_tiles + PIPELINE_LAG,),
        in_specs=[pl.BlockSpec((TILE, D_MODEL), lambda i: (jnp.minimum(i, n_tiles - 1), 0)),
                  pl.BlockSpec((1, D_MODEL, N_MEM), lambda i: (mem_batch(i), 0, 0)),
                  pl.BlockSpec((1, N_MEM, D_MODEL), lambda i: (mem_batch(i), 0, 0))]
                 + [resident(c) for c in consts],
        out_specs=pl.BlockSpec((TILE, D_MODEL), lambda i: (jnp.maximum(i - PIPELINE_LAG, 0), 0)),
        out_shape=jax.ShapeDtypeStruct(xf.shape, F32),
        scratch_shapes=[
            pltpu.VMEM((BLOCK, KV_WIDTH), F32),
            pltpu.VMEM((BLOCK, KV_WIDTH), F32),
            pltpu.VMEM((SUBLANES, TILE + CONV_HALO, CONV_CH), F32),
            pltpu.VMEM((TILE, D_MODEL), BF16),
            pltpu.VMEM((TILE, D_MODEL), BF16),
            pltpu.VMEM((TILE, D_MODEL), F32),
            pltpu.VMEM((TILE, D_MODEL), F32),
        ],
        compiler_params=pltpu.CompilerParams(
            dimension_semantics=("arbitrary",), vmem_limit_bytes=VMEM_LIMIT_BYTES),
        name="layer",
    )(xf, kt, v, *consts)
    return out.reshape(x.shape)


def _memkv_kernel(mem_ref, gmem_ref, wkv_ref, gk_ref, kt_ref, v_ref):
    mem_n = _rms(mem_ref[0], gmem_ref[...]).astype(BF16)
    kv = _dot(mem_n, wkv_ref[...])
    for hd in range(N_X_HEADS):
        cols = slice(hd * X_HEAD_DIM, (hd + 1) * X_HEAD_DIM)
        kt_ref[0, cols, :] = _rms(kv[:, cols], gk_ref[...]).T.astype(BF16)
    v_ref[0] = kv[:, D_MODEL:].astype(BF16)


def _memkv(mem, p):
    batch = mem.shape[0]
    consts = [p["gmem"], p["wkv_x"], p["gxk"]]
    return pl.pallas_call(
        _memkv_kernel,
        grid=(batch,),
        in_specs=[pl.BlockSpec((1, N_MEM, D_MODEL), lambda b: (b, 0, 0))]
                 + [pl.BlockSpec(c.shape, lambda b: (0, 0)) for c in consts],
        out_specs=[pl.BlockSpec((1, D_MODEL, N_MEM), lambda b: (b, 0, 0)),
                   pl.BlockSpec((1, N_MEM, D_MODEL), lambda b: (b, 0, 0))],
        out_shape=[jax.ShapeDtypeStruct((batch, D_MODEL, N_MEM), BF16),
                   jax.ShapeDtypeStruct((batch, N_MEM, D_MODEL), BF16)],
        compiler_params=pltpu.CompilerParams(
            dimension_semantics=("arbitrary",), vmem_limit_bytes=VMEM_LIMIT_BYTES),
        name="memkv",
    )(mem, *consts)


def _q_head_order():
    cols = []
    for j in range(Q_PER_KV):
        for c in range(N_KV_HEADS):
            hd = j + Q_PER_KV * c
            cols.extend(range(hd * HEAD_DIM, (hd + 1) * HEAD_DIM))
    return np.asarray(cols, dtype=np.int32)


def _swa_bias(sinks):
    qi = np.arange(BLOCK)[:, None]
    sk = np.arange(2 * BLOCK)[None, :]
    dist = qi + BLOCK - sk
    valid = (dist >= 0) & (dist < BLOCK)
    slopes = 2.0 ** (-8.0 * (np.arange(N_Q_HEADS, dtype=np.float32) + 1.0) / N_Q_HEADS)
    table = np.empty((2, Q_PER_KV * BLOCK, N_KV_HEADS * 2 * BLOCK), np.float32)
    for first in range(2):
        ok = valid & (sk >= BLOCK) if first else valid
        for j in range(Q_PER_KV):
            for c in range(N_KV_HEADS):
                slope = slopes[j + Q_PER_KV * c]
                blk = np.where(ok, -slope * LOG2E * dist.astype(np.float32), np.float32(NEG))
                table[first, j * BLOCK:(j + 1) * BLOCK, c * 2 * BLOCK:(c + 1) * 2 * BLOCK] = blk
    assert not valid[:, 0].any()
    head_of_row = np.repeat(np.arange(Q_PER_KV), BLOCK)
    table = jnp.asarray(table)
    for c in range(N_KV_HEADS):
        logit = sinks[head_of_row + Q_PER_KV * c].astype(F32) * LOG2E
        table = table.at[:, :, c * 2 * BLOCK].set(jnp.broadcast_to(logit, (2, Q_PER_KV * BLOCK)))
    return table


def _layer_params(l, norm_mix_g, w_in, q_norm_g, k_norm_g, sinks, conv_w, conv_b, conv_ln_g, conv_ln_b,
                  w_out, norm_x_g, norm_mem_g, wq_x, wkv_x, xq_norm_g, xk_norm_g, wo_x, norm_ffn_g,
                  w_gate_up, w_down):
    order = _q_head_order()
    row = lambda a: a.reshape(1, -1).astype(F32)
    w_in_l = w_in[l]
    w_out_l = w_out[l]
    return dict(
        gmix=row(norm_mix_g[l]),
        wq=w_in_l[:, :ATTN_WIDTH][:, order].astype(BF16),
        wkv=w_in_l[:, ATTN_WIDTH:ATTN_WIDTH + 2 * KV_WIDTH].astype(BF16),
        wc=w_in_l[:, ATTN_WIDTH + 2 * KV_WIDTH:].astype(BF16),
        gq=row(jnp.tile(q_norm_g[l], N_Q_HEADS)) * (HEAD_DIM ** -0.5 * LOG2E),
        gk=row(jnp.tile(k_norm_g[l], N_KV_HEADS)),
        bias=_swa_bias(sinks[l]),
        cw=jnp.broadcast_to(conv_w[l].astype(F32)[:, None, :], (CONV_K, SUBLANES, CONV_CH)),
        cb=row(conv_b[l]),
        lng=row(conv_ln_g[l]),
        lnb=row(conv_ln_b[l]),
        wout=jnp.concatenate([w_out_l[:ATTN_WIDTH][order], w_out_l[ATTN_WIDTH:]], axis=0).astype(BF16),
        gx=row(norm_x_g[l]),
        gmem=row(norm_mem_g[l]),
        wq_x=wq_x[l].astype(BF16),
        wkv_x=wkv_x[l].astype(BF16),
        gxq=row(xq_norm_g[l]) * (X_HEAD_DIM ** -0.5 * LOG2E),
        gxk=row(xk_norm_g[l]),
        wo_x=wo_x[l].astype(BF16),
        gffn=row(norm_ffn_g[l]),
        wg=w_gate_up[l][:, :D_FF].astype(BF16),
        wu=w_gate_up[l][:, D_FF:].astype(BF16),
        wd=w_down[l].astype(BF16),
    )


def kernel(x, mem, norm_mix_g, w_in, q_norm_g, k_norm_g, sinks, conv_w, conv_b, conv_ln_g, conv_ln_b, w_out, norm_x_g, norm_mem_g, wq_x, wkv_x, xq_norm_g, xk_norm_g, wo_x, norm_ffn_g, w_gate_up, w_down):
    depth = w_in.shape[0]
    for l in range(depth):
        p = _layer_params(l, norm_mix_g, w_in, q_norm_g, k_norm_g, sinks, conv_w, conv_b, conv_ln_g,
                          conv_ln_b, w_out, norm_x_g, norm_mem_g, wq_x, wkv_x, xq_norm_g, xk_norm_g,
                          wo_x, norm_ffn_g, w_gate_up, w_down)
        kt, v = _memkv(mem, p)
        x = _layer(x, kt, v, p)
    return x
```

```python
import functools

import jax
import jax.numpy as jnp
import numpy as np
from jax import lax
from jax.experimental import pallas as pl
from jax.experimental.pallas import tpu as pltpu

D_MODEL = 1024
N_MEM = 256
HEAD_DIM = 64
N_Q_HEADS = 8
N_KV_HEADS = 2
Q_PER_KV = N_Q_HEADS // N_KV_HEADS
ATTN_WIDTH = N_Q_HEADS * HEAD_DIM
KV_WIDTH = N_KV_HEADS * HEAD_DIM
CONV_CH = D_MODEL - ATTN_WIDTH
CONV_K = 31
BLOCK = 128
N_X_HEADS = 4
X_HEAD_DIM = D_MODEL // N_X_HEADS
D_FF = 2816
EPS = 1e-6
NEG = -1e30

SUBLANES = 8
CONV_HALO = 32
CONV_ROWS = 32
LOG2E = 1.4426950408889634
TILE = 256
PIPELINE_LAG = 2
FFN_CHUNKS = ((0, 768), (768, 1536), (1536, 2304), (2304, D_FF))
VMEM_LIMIT_BYTES = 60 * 1024 * 1024

IN_Q = (0, ATTN_WIDTH)
IN_KV = (ATTN_WIDTH, ATTN_WIDTH + 2 * KV_WIDTH)
IN_VAL = (IN_KV[1], IN_KV[1] + CONV_CH)
IN_GATE = (IN_VAL[1], IN_VAL[1] + CONV_CH)

V_GMIX, V_GX, V_GFFN, V_GQ, V_GK, V_CB, V_LNG, V_LNB, V_GXQ, V_GMEM, V_GXK = range(11)
N_VEC_ROWS = 16

BF16 = jnp.bfloat16
F32 = jnp.float32


def _dot(a, b):
    return jnp.dot(a, b, preferred_element_type=F32)


def _dot_nt(a, b):
    return lax.dot_general(a, b, (((1,), (1,)), ((), ())), preferred_element_type=F32)


def _rms(x, g):
    return x * lax.rsqrt(jnp.mean(x * x, axis=-1, keepdims=True) + EPS) * g


def _sigmoid(x):
    return 1.0 / (1.0 + jnp.exp(-x))


def _vec(vec_ref, row, width=D_MODEL):
    return vec_ref[0, row:row + 1, :width]


def _cols(w_ref, span):
    return w_ref[0, :, span[0]:span[1]]


def _head_pair_rms(x, gain):
    lane = lax.broadcasted_iota(jnp.int32, (x.shape[0], 2 * HEAD_DIM), 1)
    first = lane < HEAD_DIM
    out = []
    for j in range(x.shape[1] // (2 * HEAD_DIM)):
        cols = slice(j * 2 * HEAD_DIM, (j + 1) * 2 * HEAD_DIM)
        blk = x[:, cols]
        sq = blk * blk
        ss_a = jnp.sum(jnp.where(first, sq, 0.0), axis=-1, keepdims=True)
        ss_b = jnp.sum(jnp.where(first, 0.0, sq), axis=-1, keepdims=True)
        inv = jnp.where(first, lax.rsqrt(ss_a * (1.0 / HEAD_DIM) + EPS), lax.rsqrt(ss_b * (1.0 / HEAD_DIM) + EPS))
        out.append(blk * inv * gain[:, cols])
    return out[0] if len(out) == 1 else jnp.concatenate(out, axis=1)


def _conv_prepare(h, w_in_ref, gsh_ref):
    tile = h.shape[0]
    val = _dot(h, _cols(w_in_ref, IN_VAL))
    gate = _dot(h, _cols(w_in_ref, IN_GATE))
    gsh_ref[0, CONV_HALO:CONV_HALO + tile, :] = val * _sigmoid(gate)
    n_rows = tile + CONV_HALO
    glu = gsh_ref[0].reshape(n_rows // SUBLANES, SUBLANES, CONV_CH)
    sub = lax.broadcasted_iota(jnp.int32, (n_rows // SUBLANES - 1, SUBLANES, CONV_CH), 1)
    for r in range(1, SUBLANES):
        merged = jnp.where(sub < r, glu[1:], glu[:-1])
        shifted = pltpu.roll(merged, SUBLANES - r, axis=1)
        gsh_ref[r, 0:n_rows - SUBLANES, :] = shifted.reshape(n_rows - SUBLANES, CONV_CH)


def _conv_rows(n0, gsh_ref, cw_ref, vec_ref, mixed_ref):
    groups = CONV_ROWS // SUBLANES
    y = jnp.broadcast_to(_vec(vec_ref, V_CB, CONV_CH), (groups, SUBLANES, CONV_CH))
    for j in range(CONV_K):
        a, r = divmod(j + CONV_HALO - (CONV_K - 1), SUBLANES)
        win = gsh_ref[r, n0 + SUBLANES * a:n0 + SUBLANES * a + CONV_ROWS, :]
        y = y + cw_ref[0, j] * win.reshape(groups, SUBLANES, CONV_CH)
    y = y.reshape(CONV_ROWS, CONV_CH)
    yc = y - jnp.mean(y, axis=-1, keepdims=True)
    yn = (yc * lax.rsqrt(jnp.mean(yc * yc, axis=-1, keepdims=True) + EPS) * _vec(vec_ref, V_LNG, CONV_CH)
          + _vec(vec_ref, V_LNB, CONV_CH))
    mixed_ref[n0:n0 + CONV_ROWS, ATTN_WIDTH:] = (yn * _sigmoid(yn)).astype(BF16)


def _swa_probs(i, qn, kn, v, first_of_seq, bias_ref, kprev_ref, vprev_ref):
    lane = lax.broadcasted_iota(jnp.int32, (2 * BLOCK, KV_WIDTH), 1)
    key = lax.broadcasted_iota(jnp.int32, (2 * BLOCK, KV_WIDTH), 0)
    head0 = (lane < HEAD_DIM) & (key > 0)
    head1 = (lane >= HEAD_DIM) & (key > 0)
    ind0 = jnp.where(lane < HEAD_DIM, 1.0, 0.0).astype(BF16)
    ind1 = jnp.where(lane < HEAD_DIM, 0.0, 1.0).astype(BF16)

    def masked(a, keep):
        return jnp.where(keep, a, 0.0).astype(BF16)

    rows = slice(i * BLOCK, (i + 1) * BLOCK)
    if i == 0:
        k_prev, v_prev = kprev_ref[...], vprev_ref[...]
        bias = bias_ref[0, jnp.where(first_of_seq, 1, 0)]
    else:
        prev = slice((i - 1) * BLOCK, i * BLOCK)
        k_prev, v_prev = kn[prev], v[prev]
        bias = bias_ref[0, 0]
    kk = jnp.concatenate([k_prev, kn[rows]], axis=0)
    vv = jnp.concatenate([v_prev, v[rows]], axis=0)
    k_st = jnp.concatenate([masked(kk, head0), masked(kk, head1)], axis=0)
    v_st = jnp.concatenate([jnp.concatenate([masked(vv, head0), ind0], axis=1),
                            jnp.concatenate([masked(vv, head1), ind1], axis=1)], axis=0)
    q_st = jnp.concatenate([qn[rows, j * KV_WIDTH:(j + 1) * KV_WIDTH] for j in range(Q_PER_KV)], axis=0)
    s = _dot_nt(q_st, k_st) + bias
    probs = []
    for c in range(N_KV_HEADS):
        sc = s[:, c * 2 * BLOCK:(c + 1) * 2 * BLOCK]
        probs.append(jnp.exp2(sc - jnp.max(sc, axis=-1, keepdims=True)).astype(BF16))
    return jnp.concatenate(probs, axis=1), v_st


def _swa_out(i, probs, v_st, mixed_ref):
    o = _dot(probs, v_st)
    o = o[:, :KV_WIDTH] / o[:, KV_WIDTH:]
    for j in range(Q_PER_KV):
        mixed_ref[i * BLOCK:(i + 1) * BLOCK, j * KV_WIDTH:(j + 1) * KV_WIDTH] = (
            o[j * BLOCK:(j + 1) * BLOCK].astype(BF16))


def _xattn_probs(hd, q, gq, kt_ref):
    cols = slice(hd * X_HEAD_DIM, (hd + 1) * X_HEAD_DIM)
    qn = _rms(q[:, cols], gq).astype(BF16)
    s = _dot(qn, kt_ref[0, cols, :])
    p = jnp.exp2(s - jnp.max(s, axis=-1, keepdims=True))
    return p.astype(BF16), 1.0 / jnp.sum(p, axis=-1, keepdims=True)


def _xattn_out(hd, p, inv, v_ref, att_ref):
    cols = slice(hd * X_HEAD_DIM, (hd + 1) * X_HEAD_DIM)
    att_ref[:, cols] = (_dot(p, v_ref[0, :, cols]) * inv).astype(BF16)


def _ffn_act(h, c0, c1, wgu_ref):
    gate = _dot(h, wgu_ref[0, :, c0:c1])
    up = _dot(h, wgu_ref[0, :, D_FF + c0:D_FF + c1])
    return (gate * _sigmoid(gate) * up).astype(BF16)


def _layer_kernel(tiles_per_seq,
                  x_ref, kt_ref, v_ref, vec_ref, w_in_ref, bias_ref, cw_ref, w_out_ref, wqx_ref, wox_ref,
                  wgu_ref, wd_ref,
                  o_ref,
                  kprev_ref, vprev_ref, gsh_ref, mixed_ref, att_ref, x1_ref, x2_ref):
    step = pl.program_id(0)
    first_of_seq = step % tiles_per_seq == 0

    @pl.when(step == 0)
    def _():
        x1_ref[...] = jnp.zeros_like(x1_ref)
        x2_ref[...] = jnp.zeros_like(x2_ref)

    @pl.when(first_of_seq)
    def _():
        kprev_ref[...] = jnp.zeros_like(kprev_ref)
        vprev_ref[...] = jnp.zeros_like(vprev_ref)
        gsh_ref[0, 0:CONV_HALO, :] = jnp.zeros((CONV_HALO, CONV_CH), F32)

    conv = functools.partial(_conv_rows, gsh_ref=gsh_ref, cw_ref=cw_ref, vec_ref=vec_ref, mixed_ref=mixed_ref)
    conv_starts = list(range(0, TILE, CONV_ROWS))

    def ffn_act(c):
        return _ffn_act(h_ffn, *FFN_CHUNKS[c], wgu_ref)

    def ffn_down(c, act):
        c0, c1 = FFN_CHUNKS[c]
        return _dot(act, wd_ref[0, c0:c1, :])

    h_ffn = _rms(x2_ref[...], _vec(vec_ref, V_GFFN)).astype(BF16)
    act = ffn_act(0)
    h_mix = _rms(x_ref[...], _vec(vec_ref, V_GMIX)).astype(BF16)
    _conv_prepare(h_mix, w_in_ref, gsh_ref)
    y = ffn_down(0, act)
    act = ffn_act(1)
    for n0 in conv_starts[0:2]:
        conv(n0)
    h_x = _rms(x1_ref[...], _vec(vec_ref, V_GX)).astype(BF16)
    qn = _head_pair_rms(_dot(h_mix, _cols(w_in_ref, IN_Q)), _vec(vec_ref, V_GQ, ATTN_WIDTH)).astype(BF16)
    kv = _dot(h_mix, _cols(w_in_ref, IN_KV))
    kn = _head_pair_rms(kv[:, :KV_WIDTH], _vec(vec_ref, V_GK, KV_WIDTH))
    v = kv[:, KV_WIDTH:]
    q_x = _dot(h_x, wqx_ref[0])
    swa_probs = functools.partial(_swa_probs, qn=qn, kn=kn, v=v, first_of_seq=first_of_seq, bias_ref=bias_ref,
                                  kprev_ref=kprev_ref, vprev_ref=vprev_ref)
    xattn_probs = functools.partial(_xattn_probs, q=q_x, gq=_vec(vec_ref, V_GXQ, X_HEAD_DIM), kt_ref=kt_ref)
    swa0 = swa_probs(0)
    xp = [xattn_probs(0), xattn_probs(1)]
    for n0 in conv_starts[2:4]:
        conv(n0)
    y = y + ffn_down(1, act)
    act = ffn_act(2)
    _swa_out(0, *swa0, mixed_ref)
    for hd in (0, 1):
        _xattn_out(hd, *xp[hd], v_ref, att_ref)
    swa1 = swa_probs(1)
    xp = [xattn_probs(2), xattn_probs(3)]
    for n0 in conv_starts[4:6]:
        conv(n0)
    y = y + ffn_down(2, act)
    act = ffn_act(3)
    _swa_out(1, *swa1, mixed_ref)
    for hd in (2, 3):
        _xattn_out(hd, *xp[hd - 2], v_ref, att_ref)
    for n0 in conv_starts[6:8]:
        conv(n0)
    gsh_ref[0, 0:CONV_HALO, :] = gsh_ref[0, TILE:TILE + CONV_HALO, :]
    x2_new = x1_ref[...] + _dot(att_ref[...], wox_ref[0])
    y = y + ffn_down(3, act)
    x1_new = x_ref[...] + _dot(mixed_ref[...], w_out_ref[0])
    kprev_ref[...] = kn[TILE - BLOCK:]
    vprev_ref[...] = v[TILE - BLOCK:]
    o_ref[...] = x2_ref[...] + y
    x2_ref[...] = x2_new
    x1_ref[...] = x1_new


def _layer_spec(a, layer, **kwargs):
    return pl.BlockSpec((1,) + a.shape[1:], lambda i: (layer,) + (0,) * (a.ndim - 1), **kwargs)


def _layer(x, kt, v, p, layer):
    batch, seq, _ = x.shape
    tiles_per_seq = seq // TILE
    n_tiles = batch * tiles_per_seq
    xf = x.reshape(batch * seq, D_MODEL)

    def mem_batch(i):
        return jnp.clip(i - 1, 0, n_tiles - 1) // tiles_per_seq

    consts = [p["vec"], p["w_in"], p["bias"], p["cw"], p["w_out"], p["wq_x"], p["wo_x"], p["w_gate_up"],
              p["w_down"]]
    out = pl.pallas_call(
        functools.partial(_layer_kernel, tiles_per_seq),
        grid=(n_tiles + PIPELINE_LAG,),
        in_specs=[pl.BlockSpec((TILE, D_MODEL), lambda i: (jnp.minimum(i, n_tiles - 1), 0)),
                  pl.BlockSpec((1, D_MODEL, N_MEM), lambda i: (mem_batch(i), 0, 0)),
                  pl.BlockSpec((1, N_MEM, D_MODEL), lambda i: (mem_batch(i), 0, 0))]
                 + [_layer_spec(c, layer, pipeline_mode=pl.Buffered(1)) for c in consts],
        out_specs=pl.BlockSpec((TILE, D_MODEL), lambda i: (jnp.maximum(i - PIPELINE_LAG, 0), 0)),
        out_shape=jax.ShapeDtypeStruct(xf.shape, F32),
        scratch_shapes=[
            pltpu.VMEM((BLOCK, KV_WIDTH), F32),
            pltpu.VMEM((BLOCK, KV_WIDTH), F32),
            pltpu.VMEM((SUBLANES, TILE + CONV_HALO, CONV_CH), F32),
            pltpu.VMEM((TILE, D_MODEL), BF16),
            pltpu.VMEM((TILE, D_MODEL), BF16),
            pltpu.VMEM((TILE, D_MODEL), F32),
            pltpu.VMEM((TILE, D_MODEL), F32),
        ],
        compiler_params=pltpu.CompilerParams(
            dimension_semantics=("arbitrary",), vmem_limit_bytes=VMEM_LIMIT_BYTES),
        name="layer",
    )(xf, kt, v, *consts)
    return out.reshape(x.shape)


def _memkv_kernel(mem_ref, vec_ref, wkv_ref, kt_ref, v_ref):
    mem_n = _rms(mem_ref[0], _vec(vec_ref, V_GMEM)).astype(BF16)
    kv = _dot(mem_n, wkv_ref[0])
    gk = _vec(vec_ref, V_GXK, X_HEAD_DIM)
    for hd in range(N_X_HEADS):
        cols = slice(hd * X_HEAD_DIM, (hd + 1) * X_HEAD_DIM)
        kt_ref[0, cols, :] = _rms(kv[:, cols], gk).T.astype(BF16)
    v_ref[0] = kv[:, D_MODEL:].astype(BF16)


def _memkv(mem, p, layer):
    batch = mem.shape[0]
    return pl.pallas_call(
        _memkv_kernel,
        grid=(batch,),
        in_specs=[pl.BlockSpec((1, N_MEM, D_MODEL), lambda b: (b, 0, 0)),
                  _layer_spec(p["vec"], layer), _layer_spec(p["wkv_x"], layer)],
        out_specs=[pl.BlockSpec((1, D_MODEL, N_MEM), lambda b: (b, 0, 0)),
                   pl.BlockSpec((1, N_MEM, D_MODEL), lambda b: (b, 0, 0))],
        out_shape=[jax.ShapeDtypeStruct((batch, D_MODEL, N_MEM), BF16),
                   jax.ShapeDtypeStruct((batch, N_MEM, D_MODEL), BF16)],
        compiler_params=pltpu.CompilerParams(
            dimension_semantics=("arbitrary",), vmem_limit_bytes=VMEM_LIMIT_BYTES),
        name="memkv",
    )(mem, p["vec"], p["wkv_x"])


def _pair_heads(a, axis):
    shape = a.shape
    a = a.reshape(shape[:axis] + (N_KV_HEADS, Q_PER_KV, HEAD_DIM) + shape[axis + 1:])
    a = jnp.swapaxes(a, axis, axis + 1)
    return a.reshape(shape)


def _swa_bias(sinks):
    qi = np.arange(BLOCK)[:, None]
    sk = np.arange(2 * BLOCK)[None, :]
    dist = qi + BLOCK - sk
    valid = (dist >= 0) & (dist < BLOCK)
    assert not valid[:, 0].any()
    slopes = 2.0 ** (-8.0 * (np.arange(N_Q_HEADS, dtype=np.float32) + 1.0) / N_Q_HEADS)
    table = np.empty((2, Q_PER_KV * BLOCK, N_KV_HEADS * 2 * BLOCK), np.float32)
    for first in range(2):
        ok = valid & (sk >= BLOCK) if first else valid
        for j in range(Q_PER_KV):
            for c in range(N_KV_HEADS):
                slope = slopes[j + Q_PER_KV * c]
                blk = np.where(ok, -slope * LOG2E * dist.astype(np.float32), np.float32(NEG))
                table[first, j * BLOCK:(j + 1) * BLOCK, c * 2 * BLOCK:(c + 1) * 2 * BLOCK] = blk
    sink_col = np.zeros(table.shape[-1], bool)
    sink_col[::2 * BLOCK] = True
    n_layers = sinks.shape[0]
    logit = sinks.astype(F32).reshape(n_layers, N_KV_HEADS, Q_PER_KV) * LOG2E
    logit = jnp.repeat(jnp.repeat(jnp.swapaxes(logit, 1, 2), BLOCK, axis=1), 2 * BLOCK, axis=2)
    return jnp.where(jnp.asarray(sink_col), logit[:, None], jnp.asarray(table)[None])


def _prepare_params(norm_mix_g, w_in, q_norm_g, k_norm_g, sinks, conv_w, conv_b, conv_ln_g, conv_ln_b, w_out,
                    norm_x_g, norm_mem_g, wq_x, wkv_x, xq_norm_g, xk_norm_g, wo_x, norm_ffn_g, w_gate_up, w_down):
    n_layers = w_in.shape[0]

    def padded(a):
        a = a.astype(F32)
        return jnp.pad(a, ((0, 0), (0, D_MODEL - a.shape[1])))

    rows = [None] * N_VEC_ROWS
    rows[V_GMIX] = padded(norm_mix_g)
    rows[V_GX] = padded(norm_x_g)
    rows[V_GFFN] = padded(norm_ffn_g)
    rows[V_GQ] = padded(jnp.tile(q_norm_g, (1, N_Q_HEADS)) * (HEAD_DIM ** -0.5 * LOG2E))
    rows[V_GK] = padded(jnp.tile(k_norm_g, (1, N_KV_HEADS)))
    rows[V_CB] = padded(conv_b)
    rows[V_LNG] = padded(conv_ln_g)
    rows[V_LNB] = padded(conv_ln_b)
    rows[V_GXQ] = padded(xq_norm_g * (X_HEAD_DIM ** -0.5 * LOG2E))
    rows[V_GMEM] = padded(norm_mem_g)
    rows[V_GXK] = padded(xk_norm_g)
    zero_row = jnp.zeros((n_layers, D_MODEL), F32)
    vec = jnp.stack([zero_row if r is None else r for r in rows], axis=1)

    w_in_b = jnp.concatenate([_pair_heads(w_in[:, :, :ATTN_WIDTH], 2), w_in[:, :, ATTN_WIDTH:]], axis=2).astype(BF16)
    w_out_b = jnp.concatenate([_pair_heads(w_out[:, :ATTN_WIDTH], 1), w_out[:, ATTN_WIDTH:]], axis=1).astype(BF16)
    return dict(
        vec=vec,
        w_in=w_in_b,
        bias=_swa_bias(sinks),
        cw=jnp.broadcast_to(conv_w.astype(F32)[:, :, None, :], (n_layers, CONV_K, SUBLANES, CONV_CH)),
        w_out=w_out_b,
        wq_x=wq_x.astype(BF16),
        wkv_x=wkv_x.astype(BF16),
        wo_x=wo_x.astype(BF16),
        w_gate_up=w_gate_up.astype(BF16),
        w_down=w_down.astype(BF16),
    )


def kernel(x, mem, norm_mix_g, w_in, q_norm_g, k_norm_g, sinks, conv_w, conv_b, conv_ln_g, conv_ln_b, w_out, norm_x_g, norm_mem_g, wq_x, wkv_x, xq_norm_g, xk_norm_g, wo_x, norm_ffn_g, w_gate_up, w_down):
    p = _prepare_params(norm_mix_g, w_in, q_norm_g, k_norm_g, sinks, conv_w, conv_b, conv_ln_g, conv_ln_b, w_out,
                        norm_x_g, norm_mem_g, wq_x, wkv_x, xq_norm_g, xk_norm_g, wo_x, norm_ffn_g, w_gate_up,
                        w_down)
    for layer in range(w_in.shape[0]):
        kt, v = _memkv(mem, p, layer)
        x = _layer(x, kt, v, p, layer)
    return x
```

```python
import functools

import jax
import jax.numpy as jnp
import numpy as np
from jax import lax
from jax.experimental import pallas as pl
from jax.experimental.pallas import tpu as pltpu

D_MODEL = 1024
N_MEM = 256
HEAD_DIM = 64
N_Q_HEADS = 8
N_KV_HEADS = 2
Q_PER_KV = N_Q_HEADS // N_KV_HEADS
ATTN_WIDTH = N_Q_HEADS * HEAD_DIM
KV_WIDTH = N_KV_HEADS * HEAD_DIM
CONV_CH = D_MODEL - ATTN_WIDTH
CONV_K = 31
BLOCK = 128
N_X_HEADS = 4
X_HEAD_DIM = D_MODEL // N_X_HEADS
D_FF = 2816
EPS = 1e-6
NEG = -1e30

SUBLANES = 8
CONV_HALO = 32
CONV_ROWS = 32
LOG2E = 1.4426950408889634
TILE = 256
XATTN_LAG = 1
PIPELINE_LAG = 2
FFN_CHUNKS = ((0, 768), (768, 1536), (1536, 2304), (2304, D_FF))
VMEM_LIMIT_BYTES = 60 * 1024 * 1024

IN_Q = (0, ATTN_WIDTH)
IN_KV = (ATTN_WIDTH, ATTN_WIDTH + 2 * KV_WIDTH)
IN_VAL = (IN_KV[1], IN_KV[1] + CONV_CH)
IN_GATE = (IN_VAL[1], IN_VAL[1] + CONV_CH)

V_GMIX, V_GX, V_GFFN, V_GQ, V_GK, V_CB, V_LNG, V_LNB, V_GXQ, V_GMEM, V_GXK = range(11)
N_VEC_ROWS = 16

BF16 = jnp.bfloat16
F32 = jnp.float32


def _dot(a, b):
    return jnp.dot(a, b, preferred_element_type=F32)


def _dot_nt(a, b):
    return lax.dot_general(a, b, (((1,), (1,)), ((), ())), preferred_element_type=F32)


def _rms(x, g):
    return x * lax.rsqrt(jnp.mean(x * x, axis=-1, keepdims=True) + EPS) * g


def _sigmoid(x):
    return 1.0 / (1.0 + jnp.exp(-x))


def _vec(vec_ref, row, width=D_MODEL):
    return vec_ref[0, row:row + 1, :width]


def _cols(w_ref, span):
    return w_ref[0, :, span[0]:span[1]]


def _head_pair_rms(x, gain):
    lane = lax.broadcasted_iota(jnp.int32, (x.shape[0], 2 * HEAD_DIM), 1)
    first = lane < HEAD_DIM
    out = []
    for j in range(x.shape[1] // (2 * HEAD_DIM)):
        cols = slice(j * 2 * HEAD_DIM, (j + 1) * 2 * HEAD_DIM)
        blk = x[:, cols]
        sq = blk * blk
        ss_a = jnp.sum(jnp.where(first, sq, 0.0), axis=-1, keepdims=True)
        ss_b = jnp.sum(jnp.where(first, 0.0, sq), axis=-1, keepdims=True)
        inv = jnp.where(first, lax.rsqrt(ss_a * (1.0 / HEAD_DIM) + EPS), lax.rsqrt(ss_b * (1.0 / HEAD_DIM) + EPS))
        out.append(blk * inv * gain[:, cols])
    return out[0] if len(out) == 1 else jnp.concatenate(out, axis=1)


def _conv_prepare(h, w_in_ref, gsh_ref):
    tile = h.shape[0]
    val = _dot(h, _cols(w_in_ref, IN_VAL))
    gate = _dot(h, _cols(w_in_ref, IN_GATE))
    gsh_ref[0, CONV_HALO:CONV_HALO + tile, :] = val * _sigmoid(gate)
    n_rows = tile + CONV_HALO
    glu = gsh_ref[0].reshape(n_rows // SUBLANES, SUBLANES, CONV_CH)
    sub = lax.broadcasted_iota(jnp.int32, (n_rows // SUBLANES - 1, SUBLANES, CONV_CH), 1)
    for r in range(1, SUBLANES):
        merged = jnp.where(sub < r, glu[1:], glu[:-1])
        shifted = pltpu.roll(merged, SUBLANES - r, axis=1)
        gsh_ref[r, 0:n_rows - SUBLANES, :] = shifted.reshape(n_rows - SUBLANES, CONV_CH)


def _conv_rows(n0, gsh_ref, cw_ref, vec_ref, mixed_ref):
    groups = CONV_ROWS // SUBLANES
    y = jnp.broadcast_to(_vec(vec_ref, V_CB, CONV_CH), (groups, SUBLANES, CONV_CH))
    for j in range(CONV_K):
        a, r = divmod(j + CONV_HALO - (CONV_K - 1), SUBLANES)
        win = gsh_ref[r, n0 + SUBLANES * a:n0 + SUBLANES * a + CONV_ROWS, :]
        y = y + cw_ref[0, j] * win.reshape(groups, SUBLANES, CONV_CH)
    y = y.reshape(CONV_ROWS, CONV_CH)
    yc = y - jnp.mean(y, axis=-1, keepdims=True)
    yn = (yc * lax.rsqrt(jnp.mean(yc * yc, axis=-1, keepdims=True) + EPS) * _vec(vec_ref, V_LNG, CONV_CH)
          + _vec(vec_ref, V_LNB, CONV_CH))
    mixed_ref[n0:n0 + CONV_ROWS, ATTN_WIDTH:] = (yn * _sigmoid(yn)).astype(BF16)


def _swa_probs(i, qn, kn, v, first_of_seq, bias_ref, kprev_ref, vprev_ref):
    lane = lax.broadcasted_iota(jnp.int32, (2 * BLOCK, KV_WIDTH), 1)
    key = lax.broadcasted_iota(jnp.int32, (2 * BLOCK, KV_WIDTH), 0)
    head0 = (lane < HEAD_DIM) & (key > 0)
    head1 = (lane >= HEAD_DIM) & (key > 0)
    ind0 = jnp.where(lane < HEAD_DIM, 1.0, 0.0).astype(BF16)
    ind1 = jnp.where(lane < HEAD_DIM, 0.0, 1.0).astype(BF16)

    def masked(a, keep):
        return jnp.where(keep, a, 0.0).astype(BF16)

    rows = slice(i * BLOCK, (i + 1) * BLOCK)
    if i == 0:
        k_prev, v_prev = kprev_ref[...], vprev_ref[...]
        bias = bias_ref[0, jnp.where(first_of_seq, 1, 0)]
    else:
        prev = slice((i - 1) * BLOCK, i * BLOCK)
        k_prev, v_prev = kn[prev], v[prev]
        bias = bias_ref[0, 0]
    kk = jnp.concatenate([k_prev, kn[rows]], axis=0)
    vv = jnp.concatenate([v_prev, v[rows]], axis=0)
    k_st = jnp.concatenate([masked(kk, head0), masked(kk, head1)], axis=0)
    v_st = jnp.concatenate([jnp.concatenate([masked(vv, head0), ind0], axis=1),
                            jnp.concatenate([masked(vv, head1), ind1], axis=1)], axis=0)
    q_st = jnp.concatenate([qn[rows, j * KV_WIDTH:(j + 1) * KV_WIDTH] for j in range(Q_PER_KV)], axis=0)
    s = _dot_nt(q_st, k_st) + bias
    probs = []
    for c in range(N_KV_HEADS):
        sc = s[:, c * 2 * BLOCK:(c + 1) * 2 * BLOCK]
        probs.append(jnp.exp2(sc - jnp.max(sc, axis=-1, keepdims=True)).astype(BF16))
    return jnp.concatenate(probs, axis=1), v_st


def _swa_out(i, probs, v_st, mixed_ref):
    o = _dot(probs, v_st)
    o = o[:, :KV_WIDTH] / o[:, KV_WIDTH:]
    for j in range(Q_PER_KV):
        mixed_ref[i * BLOCK:(i + 1) * BLOCK, j * KV_WIDTH:(j + 1) * KV_WIDTH] = (
            o[j * BLOCK:(j + 1) * BLOCK].astype(BF16))


def _xattn_probs(hd, q, gq, kt_ref):
    cols = slice(hd * X_HEAD_DIM, (hd + 1) * X_HEAD_DIM)
    qn = _rms(q[:, cols], gq).astype(BF16)
    s = _dot(qn, kt_ref[0, cols, :])
    p = jnp.exp2(s - jnp.max(s, axis=-1, keepdims=True))
    return p.astype(BF16), 1.0 / jnp.sum(p, axis=-1, keepdims=True)


def _xattn_out(hd, p, inv, v_ref, att_ref):
    cols = slice(hd * X_HEAD_DIM, (hd + 1) * X_HEAD_DIM)
    att_ref[:, cols] = (_dot(p, v_ref[0, :, cols]) * inv).astype(BF16)


def _ffn_act(h, c0, c1, wgu_ref):
    gate = _dot(h, wgu_ref[0, :, c0:c1])
    up = _dot(h, wgu_ref[0, :, D_FF + c0:D_FF + c1])
    return (gate * _sigmoid(gate) * up).astype(BF16)


def _layer_kernel(tiles_per_seq, n_tiles,
                  x_ref, kt_ref, v_ref, vec_ref, w_in_ref, bias_ref, cw_ref, w_out_ref, wqx_ref, wox_ref,
                  wgu_ref, wd_ref,
                  o_ref,
                  kprev_ref, vprev_ref, gsh_ref, mixed_ref, att_ref, x1_ref, x2_ref):
    step = pl.program_id(0)
    first_of_seq = step % tiles_per_seq == 0

    @pl.when(first_of_seq)
    def _():
        kprev_ref[...] = jnp.zeros_like(kprev_ref)
        vprev_ref[...] = jnp.zeros_like(vprev_ref)
        gsh_ref[0, 0:CONV_HALO, :] = jnp.zeros((CONV_HALO, CONV_CH), F32)

    conv = functools.partial(_conv_rows, gsh_ref=gsh_ref, cw_ref=cw_ref, vec_ref=vec_ref, mixed_ref=mixed_ref)
    conv_starts = list(range(0, TILE, CONV_ROWS))

    def run(mixer, xattn, ffn):
        def ffn_act(c):
            return _ffn_act(h_ffn, *FFN_CHUNKS[c], wgu_ref)

        def ffn_down(c, act):
            c0, c1 = FFN_CHUNKS[c]
            return _dot(act, wd_ref[0, c0:c1, :])

        if ffn:
            h_ffn = _rms(x2_ref[...], _vec(vec_ref, V_GFFN)).astype(BF16)
            act = ffn_act(0)
        if mixer:
            h_mix = _rms(x_ref[...], _vec(vec_ref, V_GMIX)).astype(BF16)
            _conv_prepare(h_mix, w_in_ref, gsh_ref)
        if ffn:
            y = ffn_down(0, act)
            act = ffn_act(1)
        if mixer:
            for n0 in conv_starts[0:2]:
                conv(n0)
        if xattn:
            h_x = _rms(x1_ref[...], _vec(vec_ref, V_GX)).astype(BF16)
        if mixer:
            qn = _head_pair_rms(_dot(h_mix, _cols(w_in_ref, IN_Q)), _vec(vec_ref, V_GQ, ATTN_WIDTH)).astype(BF16)
            kv = _dot(h_mix, _cols(w_in_ref, IN_KV))
            kn = _head_pair_rms(kv[:, :KV_WIDTH], _vec(vec_ref, V_GK, KV_WIDTH))
            v = kv[:, KV_WIDTH:]
            swa_probs = functools.partial(_swa_probs, qn=qn, kn=kn, v=v, first_of_seq=first_of_seq,
                                          bias_ref=bias_ref, kprev_ref=kprev_ref, vprev_ref=vprev_ref)
        if xattn:
            q_x = _dot(h_x, wqx_ref[0])
            xattn_probs = functools.partial(_xattn_probs, q=q_x, gq=_vec(vec_ref, V_GXQ, X_HEAD_DIM), kt_ref=kt_ref)
        if mixer:
            swa0 = swa_probs(0)
        if xattn:
            xp = [xattn_probs(0), xattn_probs(1)]
        if mixer:
            for n0 in conv_starts[2:4]:
                conv(n0)
        if ffn:
            y = y + ffn_down(1, act)
            act = ffn_act(2)
        if mixer:
            _swa_out(0, *swa0, mixed_ref)
        if xattn:
            for hd in (0, 1):
                _xattn_out(hd, *xp[hd], v_ref, att_ref)
        if mixer:
            swa1 = swa_probs(1)
        if xattn:
            xp = [xattn_probs(2), xattn_probs(3)]
        if mixer:
            for n0 in conv_starts[4:6]:
                conv(n0)
        if ffn:
            y = y + ffn_down(2, act)
            act = ffn_act(3)
        if mixer:
            _swa_out(1, *swa1, mixed_ref)
        if xattn:
            for hd in (2, 3):
                _xattn_out(hd, *xp[hd - 2], v_ref, att_ref)
        if mixer:
            for n0 in conv_starts[6:8]:
                conv(n0)
            gsh_ref[0, 0:CONV_HALO, :] = gsh_ref[0, TILE:TILE + CONV_HALO, :]
        if xattn:
            x2_new = x1_ref[...] + _dot(att_ref[...], wox_ref[0])
        if ffn:
            y = y + ffn_down(3, act)
        if mixer:
            x1_new = x_ref[...] + _dot(mixed_ref[...], w_out_ref[0])
            kprev_ref[...] = kn[TILE - BLOCK:]
            vprev_ref[...] = v[TILE - BLOCK:]
        if ffn:
            o_ref[...] = x2_ref[...] + y
        if xattn:
            x2_ref[...] = x2_new
        if mixer:
            x1_ref[...] = x1_new

    in_mixer = step < n_tiles
    in_xattn = (step >= XATTN_LAG) & (step < n_tiles + XATTN_LAG)
    in_ffn = step >= PIPELINE_LAG
    for mixer, xattn, ffn in ((True, False, False), (True, True, False), (True, True, True),
                              (False, True, True), (False, False, True)):
        active = (in_mixer == mixer) & (in_xattn == xattn) & (in_ffn == ffn)
        pl.when(active)(functools.partial(run, mixer, xattn, ffn))


def _layer_spec(a, layer, **kwargs):
    return pl.BlockSpec((1,) + a.shape[1:], lambda i: (layer,) + (0,) * (a.ndim - 1), **kwargs)


def _layer(x, kt, v, p, layer):
    batch, seq, _ = x.shape
    tiles_per_seq = seq // TILE
    n_tiles = batch * tiles_per_seq
    xf = x.reshape(batch * seq, D_MODEL)

    def mem_batch(i):
        return jnp.clip(i - XATTN_LAG, 0, n_tiles - 1) // tiles_per_seq

    consts = [p["vec"], p["w_in"], p["bias"], p["cw"], p["w_out"], p["wq_x"], p["wo_x"], p["w_gate_up"],
              p["w_down"]]
    out = pl.pallas_call(
        functools.partial(_layer_kernel, tiles_per_seq, n_tiles),
        grid=(n_tiles + PIPELINE_LAG,),
        in_specs=[pl.BlockSpec((TILE, D_MODEL), lambda i: (jnp.minimum(i, n_tiles - 1), 0)),
                  pl.BlockSpec((1, D_MODEL, N_MEM), lambda i: (mem_batch(i), 0, 0)),
                  pl.BlockSpec((1, N_MEM, D_MODEL), lambda i: (mem_batch(i), 0, 0))]
                 + [_layer_spec(c, layer, pipeline_mode=pl.Buffered(1)) for c in consts],
        out_specs=pl.BlockSpec((TILE, D_MODEL), lambda i: (jnp.maximum(i - PIPELINE_LAG, 0), 0)),
        out_shape=jax.ShapeDtypeStruct(xf.shape, F32),
        scratch_shapes=[
            pltpu.VMEM((BLOCK, KV_WIDTH), F32),
            pltpu.VMEM((BLOCK, KV_WIDTH), F32),
            pltpu.VMEM((SUBLANES, TILE + CONV_HALO, CONV_CH), F32),
            pltpu.VMEM((TILE, D_MODEL), BF16),
            pltpu.VMEM((TILE, D_MODEL), BF16),
            pltpu.VMEM((TILE, D_MODEL), F32),
            pltpu.VMEM((TILE, D_MODEL), F32),
        ],
        compiler_params=pltpu.CompilerParams(
            dimension_semantics=("arbitrary",), vmem_limit_bytes=VMEM_LIMIT_BYTES),
        name="layer",
    )(xf, kt, v, *consts)
    return out.reshape(x.shape)


def _memkv_kernel(mem_ref, vec_ref, wkv_ref, kt_ref, v_ref):
    mem_n = _rms(mem_ref[0], _vec(vec_ref, V_GMEM)).astype(BF16)
    kv = _dot(mem_n, wkv_ref[0])
    gk = _vec(vec_ref, V_GXK, X_HEAD_DIM)
    for hd in range(N_X_HEADS):
        cols = slice(hd * X_HEAD_DIM, (hd + 1) * X_HEAD_DIM)
        kt_ref[0, cols, :] = _rms(kv[:, cols], gk).T.astype(BF16)
    v_ref[0] = kv[:, D_MODEL:].astype(BF16)


def _memkv(mem, p, layer):
    batch = mem.shape[0]
    return pl.pallas_call(
        _memkv_kernel,
        grid=(batch,),
        in_specs=[pl.BlockSpec((1, N_MEM, D_MODEL), lambda b: (b, 0, 0)),
                  _layer_spec(p["vec"], layer), _layer_spec(p["wkv_x"], layer)],
        out_specs=[pl.BlockSpec((1, D_MODEL, N_MEM), lambda b: (b, 0, 0)),
                   pl.BlockSpec((1, N_MEM, D_MODEL), lambda b: (b, 0, 0))],
        out_shape=[jax.ShapeDtypeStruct((batch, D_MODEL, N_MEM), BF16),
                   jax.ShapeDtypeStruct((batch, N_MEM, D_MODEL), BF16)],
        compiler_params=pltpu.CompilerParams(
            dimension_semantics=("arbitrary",), vmem_limit_bytes=VMEM_LIMIT_BYTES),
        name="memkv",
    )(mem, p["vec"], p["wkv_x"])


def _pair_heads(a, axis):
    shape = a.shape
    a = a.reshape(shape[:axis] + (N_KV_HEADS, Q_PER_KV, HEAD_DIM) + shape[axis + 1:])
    a = jnp.swapaxes(a, axis, axis + 1)
    return a.reshape(shape)


def _swa_bias(sinks):
    qi = np.arange(BLOCK)[:, None]
    sk = np.arange(2 * BLOCK)[None, :]
    dist = qi + BLOCK - sk
    valid = (dist >= 0) & (dist < BLOCK)
    assert not valid[:, 0].any()
    slopes = 2.0 ** (-8.0 * (np.arange(N_Q_HEADS, dtype=np.float32) + 1.0) / N_Q_HEADS)
    table = np.empty((2, Q_PER_KV * BLOCK, N_KV_HEADS * 2 * BLOCK), np.float32)
    for first in range(2):
        ok = valid & (sk >= BLOCK) if first else valid
        for j in range(Q_PER_KV):
            for c in range(N_KV_HEADS):
                slope = slopes[j + Q_PER_KV * c]
                blk = np.where(ok, -slope * LOG2E * dist.astype(np.float32), np.float32(NEG))
                table[first, j * BLOCK:(j + 1) * BLOCK, c * 2 * BLOCK:(c + 1) * 2 * BLOCK] = blk
    sink_col = np.zeros(table.shape[-1], bool)
    sink_col[::2 * BLOCK] = True
    n_layers = sinks.shape[0]
    logit = sinks.astype(F32).reshape(n_layers, N_KV_HEADS, Q_PER_KV) * LOG2E
    logit = jnp.repeat(jnp.repeat(jnp.swapaxes(logit, 1, 2), BLOCK, axis=1), 2 * BLOCK, axis=2)
    return jnp.where(jnp.asarray(sink_col), logit[:, None], jnp.asarray(table)[None])


def _prepare_params(norm_mix_g, w_in, q_norm_g, k_norm_g, sinks, conv_w, conv_b, conv_ln_g, conv_ln_b, w_out,
                    norm_x_g, norm_mem_g, wq_x, wkv_x, xq_norm_g, xk_norm_g, wo_x, norm_ffn_g, w_gate_up, w_down):
    n_layers = w_in.shape[0]

    def padded(a):
        a = a.astype(F32)
        return jnp.pad(a, ((0, 0), (0, D_MODEL - a.shape[1])))

    rows = [None] * N_VEC_ROWS
    rows[V_GMIX] = padded(norm_mix_g)
    rows[V_GX] = padded(norm_x_g)
    rows[V_GFFN] = padded(norm_ffn_g)
    rows[V_GQ] = padded(jnp.tile(q_norm_g, (1, N_Q_HEADS)) * (HEAD_DIM ** -0.5 * LOG2E))
    rows[V_GK] = padded(jnp.tile(k_norm_g, (1, N_KV_HEADS)))
    rows[V_CB] = padded(conv_b)
    rows[V_LNG] = padded(conv_ln_g)
    rows[V_LNB] = padded(conv_ln_b)
    rows[V_GXQ] = padded(xq_norm_g * (X_HEAD_DIM ** -0.5 * LOG2E))
    rows[V_GMEM] = padded(norm_mem_g)
    rows[V_GXK] = padded(xk_norm_g)
    zero_row = jnp.zeros((n_layers, D_MODEL), F32)
    vec = jnp.stack([zero_row if r is None else r for r in rows], axis=1)

    w_in_b = jnp.concatenate([_pair_heads(w_in[:, :, :ATTN_WIDTH], 2), w_in[:, :, ATTN_WIDTH:]], axis=2).astype(BF16)
    w_out_b = jnp.concatenate([_pair_heads(w_out[:, :ATTN_WIDTH], 1), w_out[:, ATTN_WIDTH:]], axis=1).astype(BF16)
    return dict(
        vec=vec,
        w_in=w_in_b,
        bias=_swa_bias(sinks),
        cw=jnp.broadcast_to(conv_w.astype(F32)[:, :, None, :], (n_layers, CONV_K, SUBLANES, CONV_CH)),
        w_out=w_out_b,
        wq_x=wq_x.astype(BF16),
        wkv_x=wkv_x.astype(BF16),
        wo_x=wo_x.astype(BF16),
        w_gate_up=w_gate_up.astype(BF16),
        w_down=w_down.astype(BF16),
    )


def kernel(x, mem, norm_mix_g, w_in, q_norm_g, k_norm_g, sinks, conv_w, conv_b, conv_ln_g, conv_ln_b, w_out, norm_x_g, norm_mem_g, wq_x, wkv_x, xq_norm_g, xk_norm_g, wo_x, norm_ffn_g, w_gate_up, w_down):
    p = _prepare_params(norm_mix_g, w_in, q_norm_g, k_norm_g, sinks, conv_w, conv_b, conv_ln_g, conv_ln_b, w_out,
                        norm_x_g, norm_mem_g, wq_x, wkv_x, xq_norm_g, xk_norm_g, wo_x, norm_ffn_g, w_gate_up,
                        w_down)
    for layer in range(w_in.shape[0]):
        kt, v = _memkv(mem, p, layer)
        x = _layer(x, kt, v, p, layer)
    return x
```

```python
import functools

import jax
import jax.numpy as jnp
import numpy as np
from jax import lax
from jax.experimental import pallas as pl
from jax.experimental.pallas import tpu as pltpu

D_MODEL = 1024
N_MEM = 256
HEAD_DIM = 64
N_Q_HEADS = 8
N_KV_HEADS = 2
Q_PER_KV = N_Q_HEADS // N_KV_HEADS
ATTN_WIDTH = N_Q_HEADS * HEAD_DIM
KV_WIDTH = N_KV_HEADS * HEAD_DIM
CONV_CH = D_MODEL - ATTN_WIDTH
CONV_K = 31
BLOCK = 128
N_X_HEADS = 4
X_HEAD_DIM = D_MODEL // N_X_HEADS
D_FF = 2816
EPS = 1e-6
NEG = -1e30

SUBLANES = 8
CONV_HALO = 32
CONV_ROWS = 32
LOG2E = 1.4426950408889634
TILE = 256
PIPELINE_LAG = 2
FFN_CHUNKS = ((0, 768), (768, 1536), (1536, 2304), (2304, D_FF))
VMEM_LIMIT_BYTES = 60 * 1024 * 1024
WEIGHT_DMAS_IN_FLIGHT = 3

IN_Q = (0, ATTN_WIDTH)
IN_KV = (ATTN_WIDTH, ATTN_WIDTH + 2 * KV_WIDTH)
IN_VAL = (IN_KV[1], IN_KV[1] + CONV_CH)
IN_GATE = (IN_VAL[1], IN_VAL[1] + CONV_CH)

V_GMIX, V_GX, V_GFFN, V_GQ, V_GK, V_CB, V_LNG, V_LNB, V_GXQ, V_GMEM, V_GXK = range(11)
N_VEC_ROWS = 16

BF16 = jnp.bfloat16
F32 = jnp.float32


def _dot(a, b):
    return jnp.dot(a, b, preferred_element_type=F32)


def _dot_nt(a, b):
    return lax.dot_general(a, b, (((1,), (1,)), ((), ())), preferred_element_type=F32)


def _rms(x, g):
    return x * lax.rsqrt(jnp.mean(x * x, axis=-1, keepdims=True) + EPS) * g


def _sigmoid(x):
    return 1.0 / (1.0 + jnp.exp(-x))


def _vec(vec_ref, row, width=D_MODEL):
    return vec_ref[0, row:row + 1, :width]


def _cols(w_ref, span):
    return w_ref[:, span[0]:span[1]]


def _head_pair_rms(x, gain):
    lane = lax.broadcasted_iota(jnp.int32, (x.shape[0], 2 * HEAD_DIM), 1)
    first = lane < HEAD_DIM
    out = []
    for j in range(x.shape[1] // (2 * HEAD_DIM)):
        cols = slice(j * 2 * HEAD_DIM, (j + 1) * 2 * HEAD_DIM)
        blk = x[:, cols]
        sq = blk * blk
        ss_a = jnp.sum(jnp.where(first, sq, 0.0), axis=-1, keepdims=True)
        ss_b = jnp.sum(jnp.where(first, 0.0, sq), axis=-1, keepdims=True)
        inv = jnp.where(first, lax.rsqrt(ss_a * (1.0 / HEAD_DIM) + EPS), lax.rsqrt(ss_b * (1.0 / HEAD_DIM) + EPS))
        out.append(blk * inv * gain[:, cols])
    return out[0] if len(out) == 1 else jnp.concatenate(out, axis=1)


def _conv_prepare(h, w_in_ref, gsh_ref):
    tile = h.shape[0]
    val = _dot(h, _cols(w_in_ref, IN_VAL))
    gate = _dot(h, _cols(w_in_ref, IN_GATE))
    gsh_ref[0, CONV_HALO:CONV_HALO + tile, :] = val * _sigmoid(gate)
    n_rows = tile + CONV_HALO
    glu = gsh_ref[0].reshape(n_rows // SUBLANES, SUBLANES, CONV_CH)
    sub = lax.broadcasted_iota(jnp.int32, (n_rows // SUBLANES - 1, SUBLANES, CONV_CH), 1)
    for r in range(1, SUBLANES):
        merged = jnp.where(sub < r, glu[1:], glu[:-1])
        shifted = pltpu.roll(merged, SUBLANES - r, axis=1)
        gsh_ref[r, 0:n_rows - SUBLANES, :] = shifted.reshape(n_rows - SUBLANES, CONV_CH)


def _conv_rows(n0, gsh_ref, cw_ref, vec_ref, mixed_ref):
    groups = CONV_ROWS // SUBLANES
    y = jnp.broadcast_to(_vec(vec_ref, V_CB, CONV_CH), (groups, SUBLANES, CONV_CH))
    for j in range(CONV_K):
        a, r = divmod(j + CONV_HALO - (CONV_K - 1), SUBLANES)
        win = gsh_ref[r, n0 + SUBLANES * a:n0 + SUBLANES * a + CONV_ROWS, :]
        y = y + cw_ref[0, j] * win.reshape(groups, SUBLANES, CONV_CH)
    y = y.reshape(CONV_ROWS, CONV_CH)
    yc = y - jnp.mean(y, axis=-1, keepdims=True)
    yn = (yc * lax.rsqrt(jnp.mean(yc * yc, axis=-1, keepdims=True) + EPS) * _vec(vec_ref, V_LNG, CONV_CH)
          + _vec(vec_ref, V_LNB, CONV_CH))
    mixed_ref[n0:n0 + CONV_ROWS, ATTN_WIDTH:] = (yn * _sigmoid(yn)).astype(BF16)


def _swa_probs(i, qn, kn, v, first_of_seq, bias_ref, kprev_ref, vprev_ref):
    lane = lax.broadcasted_iota(jnp.int32, (2 * BLOCK, KV_WIDTH), 1)
    key = lax.broadcasted_iota(jnp.int32, (2 * BLOCK, KV_WIDTH), 0)
    head0 = (lane < HEAD_DIM) & (key > 0)
    head1 = (lane >= HEAD_DIM) & (key > 0)
    ind0 = jnp.where(lane < HEAD_DIM, 1.0, 0.0).astype(BF16)
    ind1 = jnp.where(lane < HEAD_DIM, 0.0, 1.0).astype(BF16)

    def masked(a, keep):
        return jnp.where(keep, a, 0.0).astype(BF16)

    rows = slice(i * BLOCK, (i + 1) * BLOCK)
    if i == 0:
        k_prev, v_prev = kprev_ref[...], vprev_ref[...]
        bias = bias_ref[0, jnp.where(first_of_seq, 1, 0)]
    else:
        prev = slice((i - 1) * BLOCK, i * BLOCK)
        k_prev, v_prev = kn[prev], v[prev]
        bias = bias_ref[0, 0]
    kk = jnp.concatenate([k_prev, kn[rows]], axis=0)
    vv = jnp.concatenate([v_prev, v[rows]], axis=0)
    k_st = jnp.concatenate([masked(kk, head0), masked(kk, head1)], axis=0)
    v_st = jnp.concatenate([jnp.concatenate([masked(vv, head0), ind0], axis=1),
                            jnp.concatenate([masked(vv, head1), ind1], axis=1)], axis=0)
    q_st = jnp.concatenate([qn[rows, j * KV_WIDTH:(j + 1) * KV_WIDTH] for j in range(Q_PER_KV)], axis=0)
    s = _dot_nt(q_st, k_st) + bias
    probs = []
    for c in range(N_KV_HEADS):
        sc = s[:, c * 2 * BLOCK:(c + 1) * 2 * BLOCK]
        probs.append(jnp.exp2(sc - jnp.max(sc, axis=-1, keepdims=True)).astype(BF16))
    return jnp.concatenate(probs, axis=1), v_st


def _swa_out(i, probs, v_st, mixed_ref):
    o = _dot(probs, v_st)
    o = o[:, :KV_WIDTH] / o[:, KV_WIDTH:]
    for j in range(Q_PER_KV):
        mixed_ref[i * BLOCK:(i + 1) * BLOCK, j * KV_WIDTH:(j + 1) * KV_WIDTH] = (
            o[j * BLOCK:(j + 1) * BLOCK].astype(BF16))


def _xattn_probs(hd, q, gq, kt_ref):
    cols = slice(hd * X_HEAD_DIM, (hd + 1) * X_HEAD_DIM)
    qn = _rms(q[:, cols], gq).astype(BF16)
    s = _dot(qn, kt_ref[0, cols, :])
    p = jnp.exp2(s - jnp.max(s, axis=-1, keepdims=True))
    return p.astype(BF16), 1.0 / jnp.sum(p, axis=-1, keepdims=True)


def _xattn_out(hd, p, inv, v_ref, att_ref):
    cols = slice(hd * X_HEAD_DIM, (hd + 1) * X_HEAD_DIM)
    att_ref[:, cols] = (_dot(p, v_ref[0, :, cols]) * inv).astype(BF16)


def _ffn_act(h, c0, c1, wgu_ref):
    gate = _dot(h, wgu_ref[:, c0:c1])
    up = _dot(h, wgu_ref[:, D_FF + c0:D_FF + c1])
    return (gate * _sigmoid(gate) * up).astype(BF16)


def _weight_chunks(layer, hbm, resident, stages):
    wq_hbm, w_in_hbm, w_out_hbm, wqx_hbm, wox_hbm, wgu_hbm, wd_hbm = hbm
    w_in_ref, w_out_ref, wqx_ref, wox_ref, wgu_ref, wd_ref = resident
    used = [0] * len(stages)
    chunks = []

    def add(src, r0, c0, dst, d0, dc0, rows, cols, ring):
        slot = used[ring] % WEIGHT_DMAS_IN_FLIGHT
        used[ring] += 1
        assert rows <= stages[ring].shape[1] and cols <= stages[ring].shape[2]
        chunks.append((src.at[layer, pl.ds(r0, rows), pl.ds(c0, cols)],
                       stages[ring].at[slot, pl.ds(0, rows), pl.ds(0, cols)],
                       dst.at[pl.ds(d0, rows), pl.ds(dc0, cols)]))

    wide, mid, narrow = range(3)
    rows_wide, rows_mid, rows_narrow = (stages[r].shape[1] for r in (wide, mid, narrow))
    for r0 in range(0, D_MODEL, rows_wide):
        add(wgu_hbm, r0, 0, wgu_ref, r0, 0, rows_wide, 2 * D_FF, wide)
    for r0 in range(0, D_FF, rows_narrow):
        add(wd_hbm, r0, 0, wd_ref, r0, 0, rows_narrow, D_MODEL, narrow)
    for r0 in range(0, D_MODEL, rows_narrow):
        add(wq_hbm, r0, 0, w_in_ref, r0, 0, rows_narrow, ATTN_WIDTH, narrow)
    for r0 in range(0, D_MODEL, rows_mid):
        add(w_in_hbm, r0, ATTN_WIDTH, w_in_ref, r0, ATTN_WIDTH, rows_mid, IN_GATE[1] - ATTN_WIDTH, mid)
    for j in range(Q_PER_KV):
        for c in range(N_KV_HEADS):
            add(w_out_hbm, (j + Q_PER_KV * c) * HEAD_DIM, 0, w_out_ref, (N_KV_HEADS * j + c) * HEAD_DIM, 0,
                HEAD_DIM, D_MODEL, narrow)
    for r0 in range(ATTN_WIDTH, D_MODEL, rows_narrow):
        add(w_out_hbm, r0, 0, w_out_ref, r0, 0, rows_narrow, D_MODEL, narrow)
    for src, dst in ((wqx_hbm, wqx_ref), (wox_hbm, wox_ref)):
        for r0 in range(0, D_MODEL, rows_narrow):
            add(src, r0, 0, dst, r0, 0, rows_narrow, D_MODEL, narrow)
    return chunks


def _load_weights(chunks, sem):
    copies = [pltpu.make_async_copy(src, stage, sem.at[k % WEIGHT_DMAS_IN_FLIGHT])
              for k, (src, stage, _) in enumerate(chunks)]
    for copy in copies[:WEIGHT_DMAS_IN_FLIGHT]:
        copy.start()
    for k, (_, stage, dst) in enumerate(chunks):
        copies[k].wait()
        dst[...] = stage[...].astype(BF16)
        if k + WEIGHT_DMAS_IN_FLIGHT < len(copies):
            copies[k + WEIGHT_DMAS_IN_FLIGHT].start()


def _layer_kernel(layer, tiles_per_seq,
                  x_ref, kt_ref, v_ref, vec_ref, bias_ref, cw_ref,
                  wq_hbm, w_in_hbm, w_out_hbm, wqx_hbm, wox_hbm, wgu_hbm, wd_hbm,
                  o_ref,
                  kprev_ref, vprev_ref, gsh_ref, mixed_ref, att_ref, x1_ref, x2_ref,
                  w_in_ref, w_out_ref, wqx_ref, wox_ref, wgu_ref, wd_ref,
                  stage_wide_ref, stage_mid_ref, stage_narrow_ref, weight_sem):
    step = pl.program_id(0)
    first_of_seq = step % tiles_per_seq == 0

    @pl.when(step == 0)
    def _():
        x1_ref[...] = jnp.zeros_like(x1_ref)
        x2_ref[...] = jnp.zeros_like(x2_ref)
        _load_weights(_weight_chunks(layer, (wq_hbm, w_in_hbm, w_out_hbm, wqx_hbm, wox_hbm, wgu_hbm, wd_hbm),
                                     (w_in_ref, w_out_ref, wqx_ref, wox_ref, wgu_ref, wd_ref),
                                     (stage_wide_ref, stage_mid_ref, stage_narrow_ref)), weight_sem)

    @pl.when(first_of_seq)
    def _():
        kprev_ref[...] = jnp.zeros_like(kprev_ref)
        vprev_ref[...] = jnp.zeros_like(vprev_ref)
        gsh_ref[0, 0:CONV_HALO, :] = jnp.zeros((CONV_HALO, CONV_CH), F32)

    conv = functools.partial(_conv_rows, gsh_ref=gsh_ref, cw_ref=cw_ref, vec_ref=vec_ref, mixed_ref=mixed_ref)
    conv_starts = list(range(0, TILE, CONV_ROWS))

    def ffn_act(c):
        return _ffn_act(h_ffn, *FFN_CHUNKS[c], wgu_ref)

    def ffn_down(c, act):
        c0, c1 = FFN_CHUNKS[c]
        return _dot(act, wd_ref[c0:c1, :])

    h_ffn = _rms(x2_ref[...], _vec(vec_ref, V_GFFN)).astype(BF16)
    act = ffn_act(0)
    h_mix = _rms(x_ref[...], _vec(vec_ref, V_GMIX)).astype(BF16)
    _conv_prepare(h_mix, w_in_ref, gsh_ref)
    y = ffn_down(0, act)
    act = ffn_act(1)
    for n0 in conv_starts[0:2]:
        conv(n0)
    h_x = _rms(x1_ref[...], _vec(vec_ref, V_GX)).astype(BF16)
    qn = _head_pair_rms(_dot(h_mix, _cols(w_in_ref, IN_Q)), _vec(vec_ref, V_GQ, ATTN_WIDTH)).astype(BF16)
    kv = _dot(h_mix, _cols(w_in_ref, IN_KV))
    kn = _head_pair_rms(kv[:, :KV_WIDTH], _vec(vec_ref, V_GK, KV_WIDTH))
    v = kv[:, KV_WIDTH:]
    q_x = _dot(h_x, wqx_ref[...])
    swa_probs = functools.partial(_swa_probs, qn=qn, kn=kn, v=v, first_of_seq=first_of_seq, bias_ref=bias_ref,
                                  kprev_ref=kprev_ref, vprev_ref=vprev_ref)
    xattn_probs = functools.partial(_xattn_probs, q=q_x, gq=_vec(vec_ref, V_GXQ, X_HEAD_DIM), kt_ref=kt_ref)
    swa0 = swa_probs(0)
    xp = [xattn_probs(0), xattn_probs(1)]
    for n0 in conv_starts[2:4]:
        conv(n0)
    y = y + ffn_down(1, act)
    act = ffn_act(2)
    _swa_out(0, *swa0, mixed_ref)
    for hd in (0, 1):
        _xattn_out(hd, *xp[hd], v_ref, att_ref)
    swa1 = swa_probs(1)
    xp = [xattn_probs(2), xattn_probs(3)]
    for n0 in conv_starts[4:6]:
        conv(n0)
    y = y + ffn_down(2, act)
    act = ffn_act(3)
    _swa_out(1, *swa1, mixed_ref)
    for hd in (2, 3):
        _xattn_out(hd, *xp[hd - 2], v_ref, att_ref)
    for n0 in conv_starts[6:8]:
        conv(n0)
    gsh_ref[0, 0:CONV_HALO, :] = gsh_ref[0, TILE:TILE + CONV_HALO, :]
    x2_new = x1_ref[...] + _dot(att_ref[...], wox_ref[...])
    y = y + ffn_down(3, act)
    x1_new = x_ref[...] + _dot(mixed_ref[...], w_out_ref[...])
    kprev_ref[...] = kn[TILE - BLOCK:]
    vprev_ref[...] = v[TILE - BLOCK:]
    o_ref[...] = x2_ref[...] + y
    x2_ref[...] = x2_new
    x1_ref[...] = x1_new


def _layer_spec(a, layer, **kwargs):
    return pl.BlockSpec((1,) + a.shape[1:], lambda i: (layer,) + (0,) * (a.ndim - 1), **kwargs)


def _layer(x, kt, v, p, layer):
    batch, seq, _ = x.shape
    tiles_per_seq = seq // TILE
    n_tiles = batch * tiles_per_seq
    xf = x.reshape(batch * seq, D_MODEL)

    def mem_batch(i):
        return jnp.clip(i - 1, 0, n_tiles - 1) // tiles_per_seq

    consts = [p["vec"], p["bias"], p["cw"]]
    weights = [p["wq"], p["w_in"], p["w_out"], p["wq_x"], p["wo_x"], p["w_gate_up"], p["w_down"]]
    out = pl.pallas_call(
        functools.partial(_layer_kernel, layer, tiles_per_seq),
        grid=(n_tiles + PIPELINE_LAG,),
        in_specs=[pl.BlockSpec((TILE, D_MODEL), lambda i: (jnp.minimum(i, n_tiles - 1), 0)),
                  pl.BlockSpec((1, D_MODEL, N_MEM), lambda i: (mem_batch(i), 0, 0)),
                  pl.BlockSpec((1, N_MEM, D_MODEL), lambda i: (mem_batch(i), 0, 0))]
                 + [_layer_spec(c, layer, pipeline_mode=pl.Buffered(1)) for c in consts]
                 + [pl.BlockSpec(memory_space=pl.ANY)] * len(weights),
        out_specs=pl.BlockSpec((TILE, D_MODEL), lambda i: (jnp.maximum(i - PIPELINE_LAG, 0), 0)),
        out_shape=jax.ShapeDtypeStruct(xf.shape, F32),
        scratch_shapes=[
            pltpu.VMEM((BLOCK, KV_WIDTH), F32),
            pltpu.VMEM((BLOCK, KV_WIDTH), F32),
            pltpu.VMEM((SUBLANES, TILE + CONV_HALO, CONV_CH), F32),
            pltpu.VMEM((TILE, D_MODEL), BF16),
            pltpu.VMEM((TILE, D_MODEL), BF16),
            pltpu.VMEM((TILE, D_MODEL), F32),
            pltpu.VMEM((TILE, D_MODEL), F32),
            pltpu.VMEM((D_MODEL, IN_GATE[1]), BF16),
            pltpu.VMEM((D_MODEL, D_MODEL), BF16),
            pltpu.VMEM((D_MODEL, D_MODEL), BF16),
            pltpu.VMEM((D_MODEL, D_MODEL), BF16),
            pltpu.VMEM((D_MODEL, 2 * D_FF), BF16),
            pltpu.VMEM((D_FF, D_MODEL), BF16),
            pltpu.VMEM((WEIGHT_DMAS_IN_FLIGHT, 64, 2 * D_FF), F32),
            pltpu.VMEM((WEIGHT_DMAS_IN_FLIGHT, 128, IN_GATE[1] - ATTN_WIDTH), F32),
            pltpu.VMEM((WEIGHT_DMAS_IN_FLIGHT, 256, D_MODEL), F32),
            pltpu.SemaphoreType.DMA((WEIGHT_DMAS_IN_FLIGHT,)),
        ],
        compiler_params=pltpu.CompilerParams(
            dimension_semantics=("arbitrary",), vmem_limit_bytes=VMEM_LIMIT_BYTES),
        name="layer",
    )(xf, kt, v, *consts, *weights)
    return out.reshape(x.shape)


def _memkv_kernel(mem_ref, vec_ref, wkv_ref, kt_ref, v_ref):
    mem_n = _rms(mem_ref[0], _vec(vec_ref, V_GMEM)).astype(BF16)
    kv = _dot(mem_n, wkv_ref[0].astype(BF16))
    gk = _vec(vec_ref, V_GXK, X_HEAD_DIM)
    for hd in range(N_X_HEADS):
        cols = slice(hd * X_HEAD_DIM, (hd + 1) * X_HEAD_DIM)
        kt_ref[0, cols, :] = _rms(kv[:, cols], gk).T.astype(BF16)
    v_ref[0] = kv[:, D_MODEL:].astype(BF16)


def _memkv(mem, p, layer):
    batch = mem.shape[0]
    return pl.pallas_call(
        _memkv_kernel,
        grid=(batch,),
        in_specs=[pl.BlockSpec((1, N_MEM, D_MODEL), lambda b: (b, 0, 0)),
                  _layer_spec(p["vec"], layer), _layer_spec(p["wkv_x"], layer)],
        out_specs=[pl.BlockSpec((1, D_MODEL, N_MEM), lambda b: (b, 0, 0)),
                   pl.BlockSpec((1, N_MEM, D_MODEL), lambda b: (b, 0, 0))],
        out_shape=[jax.ShapeDtypeStruct((batch, D_MODEL, N_MEM), BF16),
                   jax.ShapeDtypeStruct((batch, N_MEM, D_MODEL), BF16)],
        compiler_params=pltpu.CompilerParams(
            dimension_semantics=("arbitrary",), vmem_limit_bytes=VMEM_LIMIT_BYTES),
        name="memkv",
    )(mem, p["vec"], p["wkv_x"])


def _pair_heads(a, axis):
    shape = a.shape
    a = a.reshape(shape[:axis] + (N_KV_HEADS, Q_PER_KV, HEAD_DIM) + shape[axis + 1:])
    a = jnp.swapaxes(a, axis, axis + 1)
    return a.reshape(shape)


def _swa_bias(sinks):
    qi = np.arange(BLOCK)[:, None]
    sk = np.arange(2 * BLOCK)[None, :]
    dist = qi + BLOCK - sk
    valid = (dist >= 0) & (dist < BLOCK)
    assert not valid[:, 0].any()
    slopes = 2.0 ** (-8.0 * (np.arange(N_Q_HEADS, dtype=np.float32) + 1.0) / N_Q_HEADS)
    table = np.empty((2, Q_PER_KV * BLOCK, N_KV_HEADS * 2 * BLOCK), np.float32)
    for first in range(2):
        ok = valid & (sk >= BLOCK) if first else valid
        for j in range(Q_PER_KV):
            for c in range(N_KV_HEADS):
                slope = slopes[j + Q_PER_KV * c]
                blk = np.where(ok, -slope * LOG2E * dist.astype(np.float32), np.float32(NEG))
                table[first, j * BLOCK:(j + 1) * BLOCK, c * 2 * BLOCK:(c + 1) * 2 * BLOCK] = blk
    sink_col = np.zeros(table.shape[-1], bool)
    sink_col[::2 * BLOCK] = True
    n_layers = sinks.shape[0]
    logit = sinks.astype(F32).reshape(n_layers, N_KV_HEADS, Q_PER_KV) * LOG2E
    logit = jnp.repeat(jnp.repeat(jnp.swapaxes(logit, 1, 2), BLOCK, axis=1), 2 * BLOCK, axis=2)
    return jnp.where(jnp.asarray(sink_col), logit[:, None], jnp.asarray(table)[None])


def _prepare_params(norm_mix_g, w_in, q_norm_g, k_norm_g, sinks, conv_w, conv_b, conv_ln_g, conv_ln_b, w_out,
                    norm_x_g, norm_mem_g, wq_x, wkv_x, xq_norm_g, xk_norm_g, wo_x, norm_ffn_g, w_gate_up, w_down):
    n_layers = w_in.shape[0]

    def padded(a):
        a = a.astype(F32)
        return jnp.pad(a, ((0, 0), (0, D_MODEL - a.shape[1])))

    rows = [None] * N_VEC_ROWS
    rows[V_GMIX] = padded(norm_mix_g)
    rows[V_GX] = padded(norm_x_g)
    rows[V_GFFN] = padded(norm_ffn_g)
    rows[V_GQ] = padded(jnp.tile(q_norm_g, (1, N_Q_HEADS)) * (HEAD_DIM ** -0.5 * LOG2E))
    rows[V_GK] = padded(jnp.tile(k_norm_g, (1, N_KV_HEADS)))
    rows[V_CB] = padded(conv_b)
    rows[V_LNG] = padded(conv_ln_g)
    rows[V_LNB] = padded(conv_ln_b)
    rows[V_GXQ] = padded(xq_norm_g * (X_HEAD_DIM ** -0.5 * LOG2E))
    rows[V_GMEM] = padded(norm_mem_g)
    rows[V_GXK] = padded(xk_norm_g)
    zero_row = jnp.zeros((n_layers, D_MODEL), F32)
    vec = jnp.stack([zero_row if r is None else r for r in rows], axis=1)

    return dict(
        vec=vec,
        bias=_swa_bias(sinks),
        cw=jnp.broadcast_to(conv_w.astype(F32)[:, :, None, :], (n_layers, CONV_K, SUBLANES, CONV_CH)),
        wq=_pair_heads(w_in[:, :, :ATTN_WIDTH], 2),
        w_in=w_in, w_out=w_out, wq_x=wq_x, wkv_x=wkv_x, wo_x=wo_x, w_gate_up=w_gate_up, w_down=w_down,
    )


def kernel(x, mem, norm_mix_g, w_in, q_norm_g, k_norm_g, sinks, conv_w, conv_b, conv_ln_g, conv_ln_b, w_out, norm_x_g, norm_mem_g, wq_x, wkv_x, xq_norm_g, xk_norm_g, wo_x, norm_ffn_g, w_gate_up, w_down):
    p = _prepare_params(norm_mix_g, w_in, q_norm_g, k_norm_g, sinks, conv_w, conv_b, conv_ln_g, conv_ln_b, w_out,
                        norm_x_g, norm_mem_g, wq_x, wkv_x, xq_norm_g, xk_norm_g, wo_x, norm_ffn_g, w_gate_up,
                        w_down)
    for layer in range(w_in.shape[0]):
        kt, v = _memkv(mem, p, layer)
        x = _layer(x, kt, v, p, layer)
    return x
```

```python
import functools

import jax
import jax.numpy as jnp
import numpy as np
from jax import lax
from jax.experimental import pallas as pl
from jax.experimental.pallas import tpu as pltpu

D_MODEL = 1024
N_MEM = 256
HEAD_DIM = 64
N_Q_HEADS = 8
N_KV_HEADS = 2
Q_PER_KV = N_Q_HEADS // N_KV_HEADS
ATTN_WIDTH = N_Q_HEADS * HEAD_DIM
KV_WIDTH = N_KV_HEADS * HEAD_DIM
CONV_CH = D_MODEL - ATTN_WIDTH
CONV_K = 31
BLOCK = 128
N_X_HEADS = 4
X_HEAD_DIM = D_MODEL // N_X_HEADS
D_FF = 2816
EPS = 1e-6
NEG = -1e30

SUBLANES = 8
CONV_HALO = 32
CONV_ROWS = 32
LOG2E = 1.4426950408889634
TILE = 256
PIPELINE_LAG = 2
FFN_CHUNKS = ((0, 768), (768, 1536), (1536, 2304), (2304, D_FF))
VMEM_LIMIT_BYTES = 60 * 1024 * 1024
WEIGHT_DMAS_IN_FLIGHT = 4

IN_Q = (0, ATTN_WIDTH)
IN_KV = (ATTN_WIDTH, ATTN_WIDTH + 2 * KV_WIDTH)
IN_VAL = (IN_KV[1], IN_KV[1] + CONV_CH)
IN_GATE = (IN_VAL[1], IN_VAL[1] + CONV_CH)

V_GMIX, V_GX, V_GFFN, V_GQ, V_GK, V_CB, V_LNG, V_LNB, V_GXQ, V_GMEM, V_GXK = range(11)
N_VEC_ROWS = 16

BF16 = jnp.bfloat16
F32 = jnp.float32


def _dot(a, b):
    return jnp.dot(a, b, preferred_element_type=F32)


def _dot_nt(a, b):
    return lax.dot_general(a, b, (((1,), (1,)), ((), ())), preferred_element_type=F32)


def _rms(x, g):
    return x * lax.rsqrt(jnp.mean(x * x, axis=-1, keepdims=True) + EPS) * g


def _sigmoid(x):
    return 1.0 / (1.0 + jnp.exp(-x))


def _vec(vec_ref, row, width=D_MODEL):
    return vec_ref[0, row:row + 1, :width]


def _cols(w_ref, span):
    return w_ref[:, span[0]:span[1]]


def _head_pair_rms(x, gain):
    lane = lax.broadcasted_iota(jnp.int32, (x.shape[0], 2 * HEAD_DIM), 1)
    first = lane < HEAD_DIM
    out = []
    for j in range(x.shape[1] // (2 * HEAD_DIM)):
        cols = slice(j * 2 * HEAD_DIM, (j + 1) * 2 * HEAD_DIM)
        blk = x[:, cols]
        sq = blk * blk
        ss_a = jnp.sum(jnp.where(first, sq, 0.0), axis=-1, keepdims=True)
        ss_b = jnp.sum(jnp.where(first, 0.0, sq), axis=-1, keepdims=True)
        inv = jnp.where(first, lax.rsqrt(ss_a * (1.0 / HEAD_DIM) + EPS), lax.rsqrt(ss_b * (1.0 / HEAD_DIM) + EPS))
        out.append(blk * inv * gain[:, cols])
    return out[0] if len(out) == 1 else jnp.concatenate(out, axis=1)


def _conv_prepare(h, w_in_ref, gsh_ref):
    tile = h.shape[0]
    val = _dot(h, _cols(w_in_ref, IN_VAL))
    gate = _dot(h, _cols(w_in_ref, IN_GATE))
    gsh_ref[0, CONV_HALO:CONV_HALO + tile, :] = val * _sigmoid(gate)
    n_rows = tile + CONV_HALO
    glu = gsh_ref[0].reshape(n_rows // SUBLANES, SUBLANES, CONV_CH)
    sub = lax.broadcasted_iota(jnp.int32, (n_rows // SUBLANES - 1, SUBLANES, CONV_CH), 1)
    for r in range(1, SUBLANES):
        merged = jnp.where(sub < r, glu[1:], glu[:-1])
        shifted = pltpu.roll(merged, SUBLANES - r, axis=1)
        gsh_ref[r, 0:n_rows - SUBLANES, :] = shifted.reshape(n_rows - SUBLANES, CONV_CH)


def _conv_rows(n0, gsh_ref, cw_ref, vec_ref, mixed_ref):
    groups = CONV_ROWS // SUBLANES
    y = jnp.broadcast_to(_vec(vec_ref, V_CB, CONV_CH), (groups, SUBLANES, CONV_CH))
    for j in range(CONV_K):
        a, r = divmod(j + CONV_HALO - (CONV_K - 1), SUBLANES)
        win = gsh_ref[r, n0 + SUBLANES * a:n0 + SUBLANES * a + CONV_ROWS, :]
        y = y + cw_ref[0, j] * win.reshape(groups, SUBLANES, CONV_CH)
    y = y.reshape(CONV_ROWS, CONV_CH)
    yc = y - jnp.mean(y, axis=-1, keepdims=True)
    yn = (yc * lax.rsqrt(jnp.mean(yc * yc, axis=-1, keepdims=True) + EPS) * _vec(vec_ref, V_LNG, CONV_CH)
          + _vec(vec_ref, V_LNB, CONV_CH))
    mixed_ref[n0:n0 + CONV_ROWS, ATTN_WIDTH:] = (yn * _sigmoid(yn)).astype(BF16)


def _swa_probs(i, qn, kn, v, first_of_seq, bias_ref, kprev_ref, vprev_ref):
    lane = lax.broadcasted_iota(jnp.int32, (2 * BLOCK, KV_WIDTH), 1)
    key = lax.broadcasted_iota(jnp.int32, (2 * BLOCK, KV_WIDTH), 0)
    head0 = (lane < HEAD_DIM) & (key > 0)
    head1 = (lane >= HEAD_DIM) & (key > 0)
    ind0 = jnp.where(lane < HEAD_DIM, 1.0, 0.0).astype(BF16)
    ind1 = jnp.where(lane < HEAD_DIM, 0.0, 1.0).astype(BF16)

    def masked(a, keep):
        return jnp.where(keep, a, 0.0).astype(BF16)

    rows = slice(i * BLOCK, (i + 1) * BLOCK)
    if i == 0:
        k_prev, v_prev = kprev_ref[...], vprev_ref[...]
        bias = bias_ref[0, jnp.where(first_of_seq, 1, 0)]
    else:
        prev = slice((i - 1) * BLOCK, i * BLOCK)
        k_prev, v_prev = kn[prev], v[prev]
        bias = bias_ref[0, 0]
    kk = jnp.concatenate([k_prev, kn[rows]], axis=0)
    vv = jnp.concatenate([v_prev, v[rows]], axis=0)
    k_st = jnp.concatenate([masked(kk, head0), masked(kk, head1)], axis=0)
    v_st = jnp.concatenate([jnp.concatenate([masked(vv, head0), ind0], axis=1),
                            jnp.concatenate([masked(vv, head1), ind1], axis=1)], axis=0)
    q_st = jnp.concatenate([qn[rows, j * KV_WIDTH:(j + 1) * KV_WIDTH] for j in range(Q_PER_KV)], axis=0)
    s = _dot_nt(q_st, k_st) + bias
    probs = []
    for c in range(N_KV_HEADS):
        sc = s[:, c * 2 * BLOCK:(c + 1) * 2 * BLOCK]
        probs.append(jnp.exp2(sc - jnp.max(sc, axis=-1, keepdims=True)).astype(BF16))
    return jnp.concatenate(probs, axis=1), v_st


def _swa_out(i, probs, v_st, mixed_ref):
    o = _dot(probs, v_st)
    o = o[:, :KV_WIDTH] / o[:, KV_WIDTH:]
    for j in range(Q_PER_KV):
        mixed_ref[i * BLOCK:(i + 1) * BLOCK, j * KV_WIDTH:(j + 1) * KV_WIDTH] = (
            o[j * BLOCK:(j + 1) * BLOCK].astype(BF16))


def _xattn_probs(hd, q, gq, kt_ref):
    cols = slice(hd * X_HEAD_DIM, (hd + 1) * X_HEAD_DIM)
    qn = _rms(q[:, cols], gq).astype(BF16)
    s = _dot(qn, kt_ref[0, cols, :])
    p = jnp.exp2(s - jnp.max(s, axis=-1, keepdims=True))
    return p.astype(BF16), 1.0 / jnp.sum(p, axis=-1, keepdims=True)


def _xattn_out(hd, p, inv, v_ref, att_ref):
    cols = slice(hd * X_HEAD_DIM, (hd + 1) * X_HEAD_DIM)
    att_ref[:, cols] = (_dot(p, v_ref[0, :, cols]) * inv).astype(BF16)


def _ffn_act(h, c0, c1, wgu_ref):
    gate = _dot(h, wgu_ref[:, c0:c1])
    up = _dot(h, wgu_ref[:, D_FF + c0:D_FF + c1])
    return (gate * _sigmoid(gate) * up).astype(BF16)


def _weight_chunks(layer, hbm, resident, stages):
    w_in_hbm, w_out_hbm, wqx_hbm, wox_hbm, wgu_hbm, wd_hbm = hbm
    w_in_ref, w_out_ref, wqx_ref, wox_ref, wgu_ref, wd_ref = resident
    used = [0] * len(stages)
    chunks = []

    def add(src, r0, c0, dst, d0, dc0, rows, cols, ring, pair_q_heads=False):
        slot = used[ring] % WEIGHT_DMAS_IN_FLIGHT
        used[ring] += 1
        assert rows <= stages[ring].shape[1] and cols <= stages[ring].shape[2]
        chunks.append((src.at[layer, pl.ds(r0, rows), pl.ds(c0, cols)],
                       stages[ring].at[slot, pl.ds(0, rows), pl.ds(0, cols)],
                       dst.at[pl.ds(d0, rows), pl.ds(dc0, cols)], pair_q_heads))

    wide, mid, narrow = range(3)
    rows_wide, rows_mid, rows_narrow = (stages[r].shape[1] for r in (wide, mid, narrow))
    for r0 in range(0, D_MODEL, rows_wide):
        add(wgu_hbm, r0, 0, wgu_ref, r0, 0, rows_wide, 2 * D_FF, wide)
    for r0 in range(0, D_FF, rows_narrow):
        add(wd_hbm, r0, 0, wd_ref, r0, 0, rows_narrow, D_MODEL, narrow)
    for r0 in range(0, D_MODEL, rows_narrow):
        add(w_in_hbm, r0, 0, w_in_ref, r0, 0, rows_narrow, ATTN_WIDTH, narrow, pair_q_heads=True)
    for r0 in range(0, D_MODEL, rows_mid):
        add(w_in_hbm, r0, ATTN_WIDTH, w_in_ref, r0, ATTN_WIDTH, rows_mid, IN_GATE[1] - ATTN_WIDTH, mid)
    for j in range(Q_PER_KV):
        for c in range(N_KV_HEADS):
            add(w_out_hbm, (j + Q_PER_KV * c) * HEAD_DIM, 0, w_out_ref, (N_KV_HEADS * j + c) * HEAD_DIM, 0,
                HEAD_DIM, D_MODEL, narrow)
    for r0 in range(ATTN_WIDTH, D_MODEL, rows_narrow):
        add(w_out_hbm, r0, 0, w_out_ref, r0, 0, rows_narrow, D_MODEL, narrow)
    for src, dst in ((wqx_hbm, wqx_ref), (wox_hbm, wox_ref)):
        for r0 in range(0, D_MODEL, rows_narrow):
            add(src, r0, 0, dst, r0, 0, rows_narrow, D_MODEL, narrow)
    return chunks


def _load_weights(chunks, sem):
    copies = [pltpu.make_async_copy(src, stage, sem.at[k % WEIGHT_DMAS_IN_FLIGHT])
              for k, (src, stage, _, _) in enumerate(chunks)]
    for copy in copies[:WEIGHT_DMAS_IN_FLIGHT]:
        copy.start()
    for k, (_, stage, dst, pair_q_heads) in enumerate(chunks):
        copies[k].wait()
        w = stage[...]
        if pair_q_heads:
            w = jnp.concatenate([w[:, (j + Q_PER_KV * c) * HEAD_DIM:(j + Q_PER_KV * c + 1) * HEAD_DIM]
                                 for j in range(Q_PER_KV) for c in range(N_KV_HEADS)], axis=1)
        dst[...] = w.astype(BF16)
        if k + WEIGHT_DMAS_IN_FLIGHT < len(copies):
            copies[k + WEIGHT_DMAS_IN_FLIGHT].start()


def _layer_kernel(layer, tiles_per_seq,
                  x_ref, kt_ref, v_ref, vec_ref, bias_ref, cw_ref,
                  w_in_hbm, w_out_hbm, wqx_hbm, wox_hbm, wgu_hbm, wd_hbm,
                  o_ref,
                  kprev_ref, vprev_ref, gsh_ref, mixed_ref, att_ref, x1_ref, x2_ref,
                  w_in_ref, w_out_ref, wqx_ref, wox_ref, wgu_ref, wd_ref,
                  stage_wide_ref, stage_mid_ref, stage_narrow_ref, weight_sem):
    step = pl.program_id(0)
    first_of_seq = step % tiles_per_seq == 0

    @pl.when(step == 0)
    def _():
        x1_ref[...] = jnp.zeros_like(x1_ref)
        x2_ref[...] = jnp.zeros_like(x2_ref)
        _load_weights(_weight_chunks(layer, (w_in_hbm, w_out_hbm, wqx_hbm, wox_hbm, wgu_hbm, wd_hbm),
                                     (w_in_ref, w_out_ref, wqx_ref, wox_ref, wgu_ref, wd_ref),
                                     (stage_wide_ref, stage_mid_ref, stage_narrow_ref)), weight_sem)

    @pl.when(first_of_seq)
    def _():
        kprev_ref[...] = jnp.zeros_like(kprev_ref)
        vprev_ref[...] = jnp.zeros_like(vprev_ref)
        gsh_ref[0, 0:CONV_HALO, :] = jnp.zeros((CONV_HALO, CONV_CH), F32)

    conv = functools.partial(_conv_rows, gsh_ref=gsh_ref, cw_ref=cw_ref, vec_ref=vec_ref, mixed_ref=mixed_ref)
    conv_starts = list(range(0, TILE, CONV_ROWS))

    def ffn_act(c):
        return _ffn_act(h_ffn, *FFN_CHUNKS[c], wgu_ref)

    def ffn_down(c, act):
        c0, c1 = FFN_CHUNKS[c]
        return _dot(act, wd_ref[c0:c1, :])

    h_ffn = _rms(x2_ref[...], _vec(vec_ref, V_GFFN)).astype(BF16)
    act = ffn_act(0)
    h_mix = _rms(x_ref[...], _vec(vec_ref, V_GMIX)).astype(BF16)
    _conv_prepare(h_mix, w_in_ref, gsh_ref)
    y = ffn_down(0, act)
    act = ffn_act(1)
    for n0 in conv_starts[0:2]:
        conv(n0)
    h_x = _rms(x1_ref[...], _vec(vec_ref, V_GX)).astype(BF16)
    qn = _head_pair_rms(_dot(h_mix, _cols(w_in_ref, IN_Q)), _vec(vec_ref, V_GQ, ATTN_WIDTH)).astype(BF16)
    kv = _dot(h_mix, _cols(w_in_ref, IN_KV))
    kn = _head_pair_rms(kv[:, :KV_WIDTH], _vec(vec_ref, V_GK, KV_WIDTH))
    v = kv[:, KV_WIDTH:]
    q_x = _dot(h_x, wqx_ref[...])
    swa_probs = functools.partial(_swa_probs, qn=qn, kn=kn, v=v, first_of_seq=first_of_seq, bias_ref=bias_ref,
                                  kprev_ref=kprev_ref, vprev_ref=vprev_ref)
    xattn_probs = functools.partial(_xattn_probs, q=q_x, gq=_vec(vec_ref, V_GXQ, X_HEAD_DIM), kt_ref=kt_ref)
    swa0 = swa_probs(0)
    xp = [xattn_probs(0), xattn_probs(1)]
    for n0 in conv_starts[2:4]:
        conv(n0)
    y = y + ffn_down(1, act)
    act = ffn_act(2)
    _swa_out(0, *swa0, mixed_ref)
    for hd in (0, 1):
        _xattn_out(hd, *xp[hd], v_ref, att_ref)
    swa1 = swa_probs(1)
    xp = [xattn_probs(2), xattn_probs(3)]
    for n0 in conv_starts[4:6]:
        conv(n0)
    y = y + ffn_down(2, act)
    act = ffn_act(3)
    _swa_out(1, *swa1, mixed_ref)
    for hd in (2, 3):
        _xattn_out(hd, *xp[hd - 2], v_ref, att_ref)
    for n0 in conv_starts[6:8]:
        conv(n0)
    gsh_ref[0, 0:CONV_HALO, :] = gsh_ref[0, TILE:TILE + CONV_HALO, :]
    x2_new = x1_ref[...] + _dot(att_ref[...], wox_ref[...])
    y = y + ffn_down(3, act)
    x1_new = x_ref[...] + _dot(mixed_ref[...], w_out_ref[...])
    kprev_ref[...] = kn[TILE - BLOCK:]
    vprev_ref[...] = v[TILE - BLOCK:]
    o_ref[...] = x2_ref[...] + y
    x2_ref[...] = x2_new
    x1_ref[...] = x1_new


def _layer_spec(a, layer, **kwargs):
    return pl.BlockSpec((1,) + a.shape[1:], lambda i: (layer,) + (0,) * (a.ndim - 1), **kwargs)


def _layer(x, kt, v, p, layer):
    batch, seq, _ = x.shape
    tiles_per_seq = seq // TILE
    n_tiles = batch * tiles_per_seq
    xf = x.reshape(batch * seq, D_MODEL)

    def mem_batch(i):
        return jnp.clip(i - 1, 0, n_tiles - 1) // tiles_per_seq

    consts = [p["vec"], p["bias"], p["cw"]]
    weights = [p["w_in"], p["w_out"], p["wq_x"], p["wo_x"], p["w_gate_up"], p["w_down"]]
    out = pl.pallas_call(
        functools.partial(_layer_kernel, layer, tiles_per_seq),
        grid=(n_tiles + PIPELINE_LAG,),
        in_specs=[pl.BlockSpec((TILE, D_MODEL), lambda i: (jnp.minimum(i, n_tiles - 1), 0)),
                  pl.BlockSpec((1, D_MODEL, N_MEM), lambda i: (mem_batch(i), 0, 0)),
                  pl.BlockSpec((1, N_MEM, D_MODEL), lambda i: (mem_batch(i), 0, 0))]
                 + [_layer_spec(c, layer, pipeline_mode=pl.Buffered(1)) for c in consts]
                 + [pl.BlockSpec(memory_space=pl.ANY)] * len(weights),
        out_specs=pl.BlockSpec((TILE, D_MODEL), lambda i: (jnp.maximum(i - PIPELINE_LAG, 0), 0)),
        out_shape=jax.ShapeDtypeStruct(xf.shape, F32),
        scratch_shapes=[
            pltpu.VMEM((BLOCK, KV_WIDTH), F32),
            pltpu.VMEM((BLOCK, KV_WIDTH), F32),
            pltpu.VMEM((SUBLANES, TILE + CONV_HALO, CONV_CH), F32),
            pltpu.VMEM((TILE, D_MODEL), BF16),
            pltpu.VMEM((TILE, D_MODEL), BF16),
            pltpu.VMEM((TILE, D_MODEL), F32),
            pltpu.VMEM((TILE, D_MODEL), F32),
            pltpu.VMEM((D_MODEL, IN_GATE[1]), BF16),
            pltpu.VMEM((D_MODEL, D_MODEL), BF16),
            pltpu.VMEM((D_MODEL, D_MODEL), BF16),
            pltpu.VMEM((D_MODEL, D_MODEL), BF16),
            pltpu.VMEM((D_MODEL, 2 * D_FF), BF16),
            pltpu.VMEM((D_FF, D_MODEL), BF16),
            pltpu.VMEM((WEIGHT_DMAS_IN_FLIGHT, 64, 2 * D_FF), F32),
            pltpu.VMEM((WEIGHT_DMAS_IN_FLIGHT, 128, IN_GATE[1] - ATTN_WIDTH), F32),
            pltpu.VMEM((WEIGHT_DMAS_IN_FLIGHT, 256, D_MODEL), F32),
            pltpu.SemaphoreType.DMA((WEIGHT_DMAS_IN_FLIGHT,)),
        ],
        compiler_params=pltpu.CompilerParams(
            dimension_semantics=("arbitrary",), vmem_limit_bytes=VMEM_LIMIT_BYTES),
        name="layer",
    )(xf, kt, v, *consts, *weights)
    return out.reshape(x.shape)


def _memkv_kernel(mem_ref, vec_ref, wkv_ref, kt_ref, v_ref):
    mem_n = _rms(mem_ref[0], _vec(vec_ref, V_GMEM)).astype(BF16)
    kv = _dot(mem_n, wkv_ref[0].astype(BF16))
    gk = _vec(vec_ref, V_GXK, X_HEAD_DIM)
    for hd in range(N_X_HEADS):
        cols = slice(hd * X_HEAD_DIM, (hd + 1) * X_HEAD_DIM)
        kt_ref[0, cols, :] = _rms(kv[:, cols], gk).T.astype(BF16)
    v_ref[0] = kv[:, D_MODEL:].astype(BF16)


def _memkv(mem, p, layer):
    batch = mem.shape[0]
    return pl.pallas_call(
        _memkv_kernel,
        grid=(batch,),
        in_specs=[pl.BlockSpec((1, N_MEM, D_MODEL), lambda b: (b, 0, 0)),
                  _layer_spec(p["vec"], layer), _layer_spec(p["wkv_x"], layer)],
        out_specs=[pl.BlockSpec((1, D_MODEL, N_MEM), lambda b: (b, 0, 0)),
                   pl.BlockSpec((1, N_MEM, D_MODEL), lambda b: (b, 0, 0))],
        out_shape=[jax.ShapeDtypeStruct((batch, D_MODEL, N_MEM), BF16),
                   jax.ShapeDtypeStruct((batch, N_MEM, D_MODEL), BF16)],
        compiler_params=pltpu.CompilerParams(
            dimension_semantics=("arbitrary",), vmem_limit_bytes=VMEM_LIMIT_BYTES),
        name="memkv",
    )(mem, p["vec"], p["wkv_x"])


def _swa_bias(sinks):
    qi = np.arange(BLOCK)[:, None]
    sk = np.arange(2 * BLOCK)[None, :]
    dist = qi + BLOCK - sk
    valid = (dist >= 0) & (dist < BLOCK)
    assert not valid[:, 0].any()
    slopes = 2.0 ** (-8.0 * (np.arange(N_Q_HEADS, dtype=np.float32) + 1.0) / N_Q_HEADS)
    table = np.empty((2, Q_PER_KV * BLOCK, N_KV_HEADS * 2 * BLOCK), np.float32)
    for first in range(2):
        ok = valid & (sk >= BLOCK) if first else valid
        for j in range(Q_PER_KV):
            for c in range(N_KV_HEADS):
                slope = slopes[j + Q_PER_KV * c]
                blk = np.where(ok, -slope * LOG2E * dist.astype(np.float32), np.float32(NEG))
                table[first, j * BLOCK:(j + 1) * BLOCK, c * 2 * BLOCK:(c + 1) * 2 * BLOCK] = blk
    sink_col = np.zeros(table.shape[-1], bool)
    sink_col[::2 * BLOCK] = True
    n_layers = sinks.shape[0]
    logit = sinks.astype(F32).reshape(n_layers, N_KV_HEADS, Q_PER_KV) * LOG2E
    logit = jnp.repeat(jnp.repeat(jnp.swapaxes(logit, 1, 2), BLOCK, axis=1), 2 * BLOCK, axis=2)
    return jnp.where(jnp.asarray(sink_col), logit[:, None], jnp.asarray(table)[None])


def _prepare_params(norm_mix_g, w_in, q_norm_g, k_norm_g, sinks, conv_w, conv_b, conv_ln_g, conv_ln_b, w_out,
                    norm_x_g, norm_mem_g, wq_x, wkv_x, xq_norm_g, xk_norm_g, wo_x, norm_ffn_g, w_gate_up, w_down):
    n_layers = w_in.shape[0]

    def padded(a):
        a = a.astype(F32)
        return jnp.pad(a, ((0, 0), (0, D_MODEL - a.shape[1])))

    rows = [None] * N_VEC_ROWS
    rows[V_GMIX] = padded(norm_mix_g)
    rows[V_GX] = padded(norm_x_g)
    rows[V_GFFN] = padded(norm_ffn_g)
    rows[V_GQ] = padded(jnp.tile(q_norm_g, (1, N_Q_HEADS)) * (HEAD_DIM ** -0.5 * LOG2E))
    rows[V_GK] = padded(jnp.tile(k_norm_g, (1, N_KV_HEADS)))
    rows[V_CB] = padded(conv_b)
    rows[V_LNG] = padded(conv_ln_g)
    rows[V_LNB] = padded(conv_ln_b)
    rows[V_GXQ] = padded(xq_norm_g * (X_HEAD_DIM ** -0.5 * LOG2E))
    rows[V_GMEM] = padded(norm_mem_g)
    rows[V_GXK] = padded(xk_norm_g)
    zero_row = jnp.zeros((n_layers, D_MODEL), F32)
    vec = jnp.stack([zero_row if r is None else r for r in rows], axis=1)

    return dict(
        vec=vec,
        bias=_swa_bias(sinks),
        cw=jnp.broadcast_to(conv_w.astype(F32)[:, :, None, :], (n_layers, CONV_K, SUBLANES, CONV_CH)),
        w_in=w_in, w_out=w_out, wq_x=wq_x, wkv_x=wkv_x, wo_x=wo_x, w_gate_up=w_gate_up, w_down=w_down,
    )


def kernel(x, mem, norm_mix_g, w_in, q_norm_g, k_norm_g, sinks, conv_w, conv_b, conv_ln_g, conv_ln_b, w_out, norm_x_g, norm_mem_g, wq_x, wkv_x, xq_norm_g, xk_norm_g, wo_x, norm_ffn_g, w_gate_up, w_down):
    p = _prepare_params(norm_mix_g, w_in, q_norm_g, k_norm_g, sinks, conv_w, conv_b, conv_ln_g, conv_ln_b, w_out,
                        norm_x_g, norm_mem_g, wq_x, wkv_x, xq_norm_g, xk_norm_g, wo_x, norm_ffn_g, w_gate_up,
                        w_down)
    for layer in range(w_in.shape[0]):
        kt, v = _memkv(mem, p, layer)
        x = _layer(x, kt, v, p, layer)
    return x
```

```python
import functools

import jax
import jax.numpy as jnp
import numpy as np
from jax import lax
from jax.experimental import pallas as pl
from jax.experimental.pallas import tpu as pltpu

D_MODEL = 1024
N_MEM = 256
HEAD_DIM = 64
N_Q_HEADS = 8
N_KV_HEADS = 2
Q_PER_KV = N_Q_HEADS // N_KV_HEADS
ATTN_WIDTH = N_Q_HEADS * HEAD_DIM
KV_WIDTH = N_KV_HEADS * HEAD_DIM
CONV_CH = D_MODEL - ATTN_WIDTH
CONV_K = 31
BLOCK = 128
N_X_HEADS = 4
X_HEAD_DIM = D_MODEL // N_X_HEADS
D_FF = 2816
EPS = 1e-6
NEG = -1e30

SUBLANES = 8
CONV_HALO = 32
CONV_ROWS = 32
LOG2E = 1.4426950408889634
TILE = 256
PIPELINE_LAG = 2
FFN_CHUNKS = ((0, 768), (768, 1536), (1536, 2304), (2304, D_FF))
VMEM_LIMIT_BYTES = 60 * 1024 * 1024
WEIGHT_DMAS_IN_FLIGHT = 4
MEMKV_BATCHES = 2

IN_Q = (0, ATTN_WIDTH)
IN_KV = (ATTN_WIDTH, ATTN_WIDTH + 2 * KV_WIDTH)
IN_VAL = (IN_KV[1], IN_KV[1] + CONV_CH)
IN_GATE = (IN_VAL[1], IN_VAL[1] + CONV_CH)

V_GMIX, V_GX, V_GFFN, V_GQ, V_GK, V_CB, V_LNG, V_LNB, V_GXQ, V_GMEM, V_GXK = range(11)
N_VEC_ROWS = 16

BF16 = jnp.bfloat16
F32 = jnp.float32


def _dot(a, b):
    return jnp.dot(a, b, preferred_element_type=F32)


def _dot_nt(a, b):
    return lax.dot_general(a, b, (((1,), (1,)), ((), ())), preferred_element_type=F32)


def _rms(x, g):
    return x * lax.rsqrt(jnp.mean(x * x, axis=-1, keepdims=True) + EPS) * g


def _sigmoid(x):
    return 1.0 / (1.0 + jnp.exp(-x))


def _vec(vec_ref, row, width=D_MODEL):
    return vec_ref[0, row:row + 1, :width]


def _cols(w_ref, span):
    return w_ref[:, span[0]:span[1]]


def _head_pair_rms(x, gain):
    lane = lax.broadcasted_iota(jnp.int32, (x.shape[0], 2 * HEAD_DIM), 1)
    first = lane < HEAD_DIM
    out = []
    for j in range(x.shape[1] // (2 * HEAD_DIM)):
        cols = slice(j * 2 * HEAD_DIM, (j + 1) * 2 * HEAD_DIM)
        blk = x[:, cols]
        sq = blk * blk
        ss_a = jnp.sum(jnp.where(first, sq, 0.0), axis=-1, keepdims=True)
        ss_b = jnp.sum(jnp.where(first, 0.0, sq), axis=-1, keepdims=True)
        inv = jnp.where(first, lax.rsqrt(ss_a * (1.0 / HEAD_DIM) + EPS), lax.rsqrt(ss_b * (1.0 / HEAD_DIM) + EPS))
        out.append(blk * inv * gain[:, cols])
    return out[0] if len(out) == 1 else jnp.concatenate(out, axis=1)


def _conv_prepare(h, w_in_ref, gsh_ref):
    tile = h.shape[0]
    val = _dot(h, _cols(w_in_ref, IN_VAL))
    gate = _dot(h, _cols(w_in_ref, IN_GATE))
    gsh_ref[0, CONV_HALO:CONV_HALO + tile, :] = val * _sigmoid(gate)
    n_rows = tile + CONV_HALO
    glu = gsh_ref[0].reshape(n_rows // SUBLANES, SUBLANES, CONV_CH)
    sub = lax.broadcasted_iota(jnp.int32, (n_rows // SUBLANES - 1, SUBLANES, CONV_CH), 1)
    for r in range(1, SUBLANES):
        merged = jnp.where(sub < r, glu[1:], glu[:-1])
        shifted = pltpu.roll(merged, SUBLANES - r, axis=1)
        gsh_ref[r, 0:n_rows - SUBLANES, :] = shifted.reshape(n_rows - SUBLANES, CONV_CH)


def _conv_rows(n0, gsh_ref, cw_ref, vec_ref, mixed_ref):
    groups = CONV_ROWS // SUBLANES
    y = jnp.broadcast_to(_vec(vec_ref, V_CB, CONV_CH), (groups, SUBLANES, CONV_CH))
    for j in range(CONV_K):
        a, r = divmod(j + CONV_HALO - (CONV_K - 1), SUBLANES)
        win = gsh_ref[r, n0 + SUBLANES * a:n0 + SUBLANES * a + CONV_ROWS, :]
        y = y + cw_ref[0, j] * win.reshape(groups, SUBLANES, CONV_CH)
    y = y.reshape(CONV_ROWS, CONV_CH)
    yc = y - jnp.mean(y, axis=-1, keepdims=True)
    yn = (yc * lax.rsqrt(jnp.mean(yc * yc, axis=-1, keepdims=True) + EPS) * _vec(vec_ref, V_LNG, CONV_CH)
          + _vec(vec_ref, V_LNB, CONV_CH))
    mixed_ref[n0:n0 + CONV_ROWS, ATTN_WIDTH:] = (yn * _sigmoid(yn)).astype(BF16)


def _swa_probs(i, qn, kn, v, first_of_seq, bias_ref, kprev_ref, vprev_ref):
    lane = lax.broadcasted_iota(jnp.int32, (2 * BLOCK, KV_WIDTH), 1)
    key = lax.broadcasted_iota(jnp.int32, (2 * BLOCK, KV_WIDTH), 0)
    head0 = (lane < HEAD_DIM) & (key > 0)
    head1 = (lane >= HEAD_DIM) & (key > 0)
    ind0 = jnp.where(lane < HEAD_DIM, 1.0, 0.0).astype(BF16)
    ind1 = jnp.where(lane < HEAD_DIM, 0.0, 1.0).astype(BF16)

    def masked(a, keep):
        return jnp.where(keep, a, 0.0).astype(BF16)

    rows = slice(i * BLOCK, (i + 1) * BLOCK)
    if i == 0:
        k_prev, v_prev = kprev_ref[...], vprev_ref[...]
        bias = bias_ref[0, jnp.where(first_of_seq, 1, 0)]
    else:
        prev = slice((i - 1) * BLOCK, i * BLOCK)
        k_prev, v_prev = kn[prev], v[prev]
        bias = bias_ref[0, 0]
    kk = jnp.concatenate([k_prev, kn[rows]], axis=0)
    vv = jnp.concatenate([v_prev, v[rows]], axis=0)
    k_st = jnp.concatenate([masked(kk, head0), masked(kk, head1)], axis=0)
    v_st = jnp.concatenate([jnp.concatenate([masked(vv, head0), ind0], axis=1),
                            jnp.concatenate([masked(vv, head1), ind1], axis=1)], axis=0)
    q_st = jnp.concatenate([qn[rows, j * KV_WIDTH:(j + 1) * KV_WIDTH] for j in range(Q_PER_KV)], axis=0)
    s = _dot_nt(q_st, k_st) + bias
    probs = []
    for c in range(N_KV_HEADS):
        sc = s[:, c * 2 * BLOCK:(c + 1) * 2 * BLOCK]
        probs.append(jnp.exp2(sc - jnp.max(sc, axis=-1, keepdims=True)).astype(BF16))
    return jnp.concatenate(probs, axis=1), v_st


def _swa_out(i, probs, v_st, mixed_ref):
    o = _dot(probs, v_st)
    o = o[:, :KV_WIDTH] / o[:, KV_WIDTH:]
    for j in range(Q_PER_KV):
        mixed_ref[i * BLOCK:(i + 1) * BLOCK, j * KV_WIDTH:(j + 1) * KV_WIDTH] = (
            o[j * BLOCK:(j + 1) * BLOCK].astype(BF16))


def _xattn_probs(hd, q, gq, kt_ref):
    cols = slice(hd * X_HEAD_DIM, (hd + 1) * X_HEAD_DIM)
    qn = _rms(q[:, cols], gq).astype(BF16)
    s = _dot(qn, kt_ref[cols, :])
    p = jnp.exp2(s - jnp.max(s, axis=-1, keepdims=True))
    return p.astype(BF16), 1.0 / jnp.sum(p, axis=-1, keepdims=True)


def _xattn_out(hd, p, inv, v_ref, att_ref):
    cols = slice(hd * X_HEAD_DIM, (hd + 1) * X_HEAD_DIM)
    att_ref[:, cols] = (_dot(p, v_ref[:, cols]) * inv).astype(BF16)


def _ffn_act(h, c0, c1, wgu_ref):
    gate = _dot(h, wgu_ref[:, c0:c1])
    up = _dot(h, wgu_ref[:, D_FF + c0:D_FF + c1])
    return (gate * _sigmoid(gate) * up).astype(BF16)


def _weight_chunks(layer, hbm, resident, stages):
    w_in_hbm, w_out_hbm, wqx_hbm, wox_hbm, wgu_hbm, wd_hbm = hbm
    w_in_ref, w_out_ref, wqx_ref, wox_ref, wgu_ref, wd_ref = resident
    used = [0] * len(stages)
    chunks = []

    def add(src, r0, c0, dst, d0, dc0, rows, cols, ring, pair_q_heads=False):
        slot = used[ring] % WEIGHT_DMAS_IN_FLIGHT
        used[ring] += 1
        assert rows <= stages[ring].shape[1] and cols <= stages[ring].shape[2]
        chunks.append((src.at[layer, pl.ds(r0, rows), pl.ds(c0, cols)],
                       stages[ring].at[slot, pl.ds(0, rows), pl.ds(0, cols)],
                       dst.at[pl.ds(d0, rows), pl.ds(dc0, cols)], pair_q_heads))

    wide, mid, narrow = range(3)
    rows_wide, rows_mid, rows_narrow = (stages[r].shape[1] for r in (wide, mid, narrow))
    for r0 in range(0, D_MODEL, rows_wide):
        add(wgu_hbm, r0, 0, wgu_ref, r0, 0, rows_wide, 2 * D_FF, wide)
    for r0 in range(0, D_FF, rows_narrow):
        add(wd_hbm, r0, 0, wd_ref, r0, 0, rows_narrow, D_MODEL, narrow)
    for r0 in range(0, D_MODEL, rows_narrow):
        add(w_in_hbm, r0, 0, w_in_ref, r0, 0, rows_narrow, ATTN_WIDTH, narrow, pair_q_heads=True)
    for r0 in range(0, D_MODEL, rows_mid):
        add(w_in_hbm, r0, ATTN_WIDTH, w_in_ref, r0, ATTN_WIDTH, rows_mid, IN_GATE[1] - ATTN_WIDTH, mid)
    for j in range(Q_PER_KV):
        for c in range(N_KV_HEADS):
            add(w_out_hbm, (j + Q_PER_KV * c) * HEAD_DIM, 0, w_out_ref, (N_KV_HEADS * j + c) * HEAD_DIM, 0,
                HEAD_DIM, D_MODEL, narrow)
    for r0 in range(ATTN_WIDTH, D_MODEL, rows_narrow):
        add(w_out_hbm, r0, 0, w_out_ref, r0, 0, rows_narrow, D_MODEL, narrow)
    for src, dst in ((wqx_hbm, wqx_ref), (wox_hbm, wox_ref)):
        for r0 in range(0, D_MODEL, rows_narrow):
            add(src, r0, 0, dst, r0, 0, rows_narrow, D_MODEL, narrow)
    return chunks


def _load_weights(chunks, sem):
    copies = [pltpu.make_async_copy(src, stage, sem.at[k % WEIGHT_DMAS_IN_FLIGHT])
              for k, (src, stage, _, _) in enumerate(chunks)]
    for copy in copies[:WEIGHT_DMAS_IN_FLIGHT]:
        copy.start()
    for k, (_, stage, dst, pair_q_heads) in enumerate(chunks):
        copies[k].wait()
        w = stage[...]
        if pair_q_heads:
            w = jnp.concatenate([w[:, (j + Q_PER_KV * c) * HEAD_DIM:(j + Q_PER_KV * c + 1) * HEAD_DIM]
                                 for j in range(Q_PER_KV) for c in range(N_KV_HEADS)], axis=1)
        dst[...] = w.astype(BF16)
        if k + WEIGHT_DMAS_IN_FLIGHT < len(copies):
            copies[k + WEIGHT_DMAS_IN_FLIGHT].start()


def _layer_kernel(layer, tiles_per_seq,
                  x_ref, kt_ref, v_ref, vec_ref, bias_ref, cw_ref,
                  w_in_hbm, w_out_hbm, wqx_hbm, wox_hbm, wgu_hbm, wd_hbm,
                  o_ref,
                  kprev_ref, vprev_ref, gsh_ref, mixed_ref, att_ref, x1_ref, x2_ref,
                  w_in_ref, w_out_ref, wqx_ref, wox_ref, wgu_ref, wd_ref,
                  stage_wide_ref, stage_mid_ref, stage_narrow_ref, weight_sem):
    step = pl.program_id(0)
    first_of_seq = step % tiles_per_seq == 0

    @pl.when(step == 0)
    def _():
        x1_ref[...] = jnp.zeros_like(x1_ref)
        x2_ref[...] = jnp.zeros_like(x2_ref)
        _load_weights(_weight_chunks(layer, (w_in_hbm, w_out_hbm, wqx_hbm, wox_hbm, wgu_hbm, wd_hbm),
                                     (w_in_ref, w_out_ref, wqx_ref, wox_ref, wgu_ref, wd_ref),
                                     (stage_wide_ref, stage_mid_ref, stage_narrow_ref)), weight_sem)

    @pl.when(first_of_seq)
    def _():
        kprev_ref[...] = jnp.zeros_like(kprev_ref)
        vprev_ref[...] = jnp.zeros_like(vprev_ref)
        gsh_ref[0, 0:CONV_HALO, :] = jnp.zeros((CONV_HALO, CONV_CH), F32)

    conv = functools.partial(_conv_rows, gsh_ref=gsh_ref, cw_ref=cw_ref, vec_ref=vec_ref, mixed_ref=mixed_ref)
    conv_starts = list(range(0, TILE, CONV_ROWS))

    def ffn_act(c):
        return _ffn_act(h_ffn, *FFN_CHUNKS[c], wgu_ref)

    def ffn_down(c, act):
        c0, c1 = FFN_CHUNKS[c]
        return _dot(act, wd_ref[c0:c1, :])

    h_ffn = _rms(x2_ref[...], _vec(vec_ref, V_GFFN)).astype(BF16)
    act = ffn_act(0)
    h_mix = _rms(x_ref[...], _vec(vec_ref, V_GMIX)).astype(BF16)
    _conv_prepare(h_mix, w_in_ref, gsh_ref)
    y = ffn_down(0, act)
    act = ffn_act(1)
    for n0 in conv_starts[0:2]:
        conv(n0)
    h_x = _rms(x1_ref[...], _vec(vec_ref, V_GX)).astype(BF16)
    qn = _head_pair_rms(_dot(h_mix, _cols(w_in_ref, IN_Q)), _vec(vec_ref, V_GQ, ATTN_WIDTH)).astype(BF16)
    kv = _dot(h_mix, _cols(w_in_ref, IN_KV))
    kn = _head_pair_rms(kv[:, :KV_WIDTH], _vec(vec_ref, V_GK, KV_WIDTH))
    v = kv[:, KV_WIDTH:]
    q_x = _dot(h_x, wqx_ref[...])
    swa_probs = functools.partial(_swa_probs, qn=qn, kn=kn, v=v, first_of_seq=first_of_seq, bias_ref=bias_ref,
                                  kprev_ref=kprev_ref, vprev_ref=vprev_ref)
    xattn_probs = functools.partial(_xattn_probs, q=q_x, gq=_vec(vec_ref, V_GXQ, X_HEAD_DIM), kt_ref=kt_ref)
    swa0 = swa_probs(0)
    xp = [xattn_probs(0), xattn_probs(1)]
    for n0 in conv_starts[2:4]:
        conv(n0)
    y = y + ffn_down(1, act)
    act = ffn_act(2)
    _swa_out(0, *swa0, mixed_ref)
    for hd in (0, 1):
        _xattn_out(hd, *xp[hd], v_ref, att_ref)
    swa1 = swa_probs(1)
    xp = [xattn_probs(2), xattn_probs(3)]
    for n0 in conv_starts[4:6]:
        conv(n0)
    y = y + ffn_down(2, act)
    act = ffn_act(3)
    _swa_out(1, *swa1, mixed_ref)
    for hd in (2, 3):
        _xattn_out(hd, *xp[hd - 2], v_ref, att_ref)
    for n0 in conv_starts[6:8]:
        conv(n0)
    gsh_ref[0, 0:CONV_HALO, :] = gsh_ref[0, TILE:TILE + CONV_HALO, :]
    x2_new = x1_ref[...] + _dot(att_ref[...], wox_ref[...])
    y = y + ffn_down(3, act)
    x1_new = x_ref[...] + _dot(mixed_ref[...], w_out_ref[...])
    kprev_ref[...] = kn[TILE - BLOCK:]
    vprev_ref[...] = v[TILE - BLOCK:]
    o_ref[...] = x2_ref[...] + y
    x2_ref[...] = x2_new
    x1_ref[...] = x1_new


def _layer_spec(a, layer, **kwargs):
    return pl.BlockSpec((1,) + a.shape[1:], lambda i: (layer,) + (0,) * (a.ndim - 1), **kwargs)


def _layer(x, kt, v, p, layer):
    batch, seq, _ = x.shape
    tiles_per_seq = seq // TILE
    n_tiles = batch * tiles_per_seq
    xf = x.reshape(batch * seq, D_MODEL)

    def mem_batch(i):
        return jnp.clip(i - 1, 0, n_tiles - 1) // tiles_per_seq

    consts = [p["vec"], p["bias"], p["cw"]]
    weights = [p["w_in"], p["w_out"], p["wq_x"], p["wo_x"], p["w_gate_up"], p["w_down"]]
    out = pl.pallas_call(
        functools.partial(_layer_kernel, layer, tiles_per_seq),
        grid=(n_tiles + PIPELINE_LAG,),
        in_specs=[pl.BlockSpec((TILE, D_MODEL), lambda i: (jnp.minimum(i, n_tiles - 1), 0)),
                  pl.BlockSpec((None, None, D_MODEL, N_MEM), lambda i: (layer, mem_batch(i), 0, 0)),
                  pl.BlockSpec((None, None, N_MEM, D_MODEL), lambda i: (layer, mem_batch(i), 0, 0))]
                 + [_layer_spec(c, layer, pipeline_mode=pl.Buffered(1)) for c in consts]
                 + [pl.BlockSpec(memory_space=pl.ANY)] * len(weights),
        out_specs=pl.BlockSpec((TILE, D_MODEL), lambda i: (jnp.maximum(i - PIPELINE_LAG, 0), 0)),
        out_shape=jax.ShapeDtypeStruct(xf.shape, F32),
        scratch_shapes=[
            pltpu.VMEM((BLOCK, KV_WIDTH), F32),
            pltpu.VMEM((BLOCK, KV_WIDTH), F32),
            pltpu.VMEM((SUBLANES, TILE + CONV_HALO, CONV_CH), F32),
            pltpu.VMEM((TILE, D_MODEL), BF16),
            pltpu.VMEM((TILE, D_MODEL), BF16),
            pltpu.VMEM((TILE, D_MODEL), F32),
            pltpu.VMEM((TILE, D_MODEL), F32),
            pltpu.VMEM((D_MODEL, IN_GATE[1]), BF16),
            pltpu.VMEM((D_MODEL, D_MODEL), BF16),
            pltpu.VMEM((D_MODEL, D_MODEL), BF16),
            pltpu.VMEM((D_MODEL, D_MODEL), BF16),
            pltpu.VMEM((D_MODEL, 2 * D_FF), BF16),
            pltpu.VMEM((D_FF, D_MODEL), BF16),
            pltpu.VMEM((WEIGHT_DMAS_IN_FLIGHT, 64, 2 * D_FF), F32),
            pltpu.VMEM((WEIGHT_DMAS_IN_FLIGHT, 128, IN_GATE[1] - ATTN_WIDTH), F32),
            pltpu.VMEM((WEIGHT_DMAS_IN_FLIGHT, 256, D_MODEL), F32),
            pltpu.SemaphoreType.DMA((WEIGHT_DMAS_IN_FLIGHT,)),
        ],
        compiler_params=pltpu.CompilerParams(
            dimension_semantics=("arbitrary",), vmem_limit_bytes=VMEM_LIMIT_BYTES),
        name="layer",
    )(xf, kt, v, *consts, *weights)
    return out.reshape(x.shape)


def _memkv_kernel(mem_ref, vec_ref, wkv_ref, kt_ref, v_ref):
    n = mem_ref.shape[0]
    mem_n = _rms(mem_ref[...].reshape(n * N_MEM, D_MODEL), _vec(vec_ref, V_GMEM)).astype(BF16)
    kv = _dot(mem_n, wkv_ref[0].astype(BF16))
    gk = _vec(vec_ref, V_GXK, X_HEAD_DIM)
    for b in range(n):
        rows = slice(b * N_MEM, (b + 1) * N_MEM)
        for hd in range(N_X_HEADS):
            cols = slice(hd * X_HEAD_DIM, (hd + 1) * X_HEAD_DIM)
            kt_ref[0, b, cols, :] = _rms(kv[rows, cols], gk).T.astype(BF16)
        v_ref[0, b] = kv[rows, D_MODEL:].astype(BF16)


def _memkv(mem, p):
    batch = mem.shape[0]
    n_layers = p["wkv_x"].shape[0]

    def by_layer(a):
        return pl.BlockSpec((1,) + a.shape[1:], lambda l, j: (l,) + (0,) * (a.ndim - 1))

    return pl.pallas_call(
        _memkv_kernel,
        grid=(n_layers, batch // MEMKV_BATCHES),
        in_specs=[pl.BlockSpec((MEMKV_BATCHES, N_MEM, D_MODEL), lambda l, j: (j, 0, 0)),
                  by_layer(p["vec"]), by_layer(p["wkv_x"])],
        out_specs=[pl.BlockSpec((1, MEMKV_BATCHES, D_MODEL, N_MEM), lambda l, j: (l, j, 0, 0)),
                   pl.BlockSpec((1, MEMKV_BATCHES, N_MEM, D_MODEL), lambda l, j: (l, j, 0, 0))],
        out_shape=[jax.ShapeDtypeStruct((n_layers, batch, D_MODEL, N_MEM), BF16),
                   jax.ShapeDtypeStruct((n_layers, batch, N_MEM, D_MODEL), BF16)],
        compiler_params=pltpu.CompilerParams(
            dimension_semantics=("arbitrary", "arbitrary"), vmem_limit_bytes=VMEM_LIMIT_BYTES),
        name="memkv",
    )(mem, p["vec"], p["wkv_x"])


def _swa_bias(sinks):
    qi = np.arange(BLOCK)[:, None]
    sk = np.arange(2 * BLOCK)[None, :]
    dist = qi + BLOCK - sk
    valid = (dist >= 0) & (dist < BLOCK)
    assert not valid[:, 0].any()
    slopes = 2.0 ** (-8.0 * (np.arange(N_Q_HEADS, dtype=np.float32) + 1.0) / N_Q_HEADS)
    table = np.empty((2, Q_PER_KV * BLOCK, N_KV_HEADS * 2 * BLOCK), np.float32)
    for first in range(2):
        ok = valid & (sk >= BLOCK) if first else valid
        for j in range(Q_PER_KV):
            for c in range(N_KV_HEADS):
                slope = slopes[j + Q_PER_KV * c]
                blk = np.where(ok, -slope * LOG2E * dist.astype(np.float32), np.float32(NEG))
                table[first, j * BLOCK:(j + 1) * BLOCK, c * 2 * BLOCK:(c + 1) * 2 * BLOCK] = blk
    sink_col = np.zeros(table.shape[-1], bool)
    sink_col[::2 * BLOCK] = True
    n_layers = sinks.shape[0]
    logit = sinks.astype(F32).reshape(n_layers, N_KV_HEADS, Q_PER_KV) * LOG2E
    logit = jnp.repeat(jnp.repeat(jnp.swapaxes(logit, 1, 2), BLOCK, axis=1), 2 * BLOCK, axis=2)
    return jnp.where(jnp.asarray(sink_col), logit[:, None], jnp.asarray(table)[None])


def _prepare_params(norm_mix_g, w_in, q_norm_g, k_norm_g, sinks, conv_w, conv_b, conv_ln_g, conv_ln_b, w_out,
                    norm_x_g, norm_mem_g, wq_x, wkv_x, xq_norm_g, xk_norm_g, wo_x, norm_ffn_g, w_gate_up, w_down):
    n_layers = w_in.shape[0]

    def padded(a):
        a = a.astype(F32)
        return jnp.pad(a, ((0, 0), (0, D_MODEL - a.shape[1])))

    rows = [None] * N_VEC_ROWS
    rows[V_GMIX] = padded(norm_mix_g)
    rows[V_GX] = padded(norm_x_g)
    rows[V_GFFN] = padded(norm_ffn_g)
    rows[V_GQ] = padded(jnp.tile(q_norm_g, (1, N_Q_HEADS)) * (HEAD_DIM ** -0.5 * LOG2E))
    rows[V_GK] = padded(jnp.tile(k_norm_g, (1, N_KV_HEADS)))
    rows[V_CB] = padded(conv_b)
    rows[V_LNG] = padded(conv_ln_g)
    rows[V_LNB] = padded(conv_ln_b)
    rows[V_GXQ] = padded(xq_norm_g * (X_HEAD_DIM ** -0.5 * LOG2E))
    rows[V_GMEM] = padded(norm_mem_g)
    rows[V_GXK] = padded(xk_norm_g)
    zero_row = jnp.zeros((n_layers, D_MODEL), F32)
    vec = jnp.stack([zero_row if r is None else r for r in rows], axis=1)

    return dict(
        vec=vec,
        bias=_swa_bias(sinks),
        cw=jnp.broadcast_to(conv_w.astype(F32)[:, :, None, :], (n_layers, CONV_K, SUBLANES, CONV_CH)),
        w_in=w_in, w_out=w_out, wq_x=wq_x, wkv_x=wkv_x, wo_x=wo_x, w_gate_up=w_gate_up, w_down=w_down,
    )


def kernel(x, mem, norm_mix_g, w_in, q_norm_g, k_norm_g, sinks, conv_w, conv_b, conv_ln_g, conv_ln_b, w_out, norm_x_g, norm_mem_g, wq_x, wkv_x, xq_norm_g, xk_norm_g, wo_x, norm_ffn_g, w_gate_up, w_down):
    p = _prepare_params(norm_mix_g, w_in, q_norm_g, k_norm_g, sinks, conv_w, conv_b, conv_ln_g, conv_ln_b, w_out,
                        norm_x_g, norm_mem_g, wq_x, wkv_x, xq_norm_g, xk_norm_g, wo_x, norm_ffn_g, w_gate_up,
                        w_down)
    kt, v = _memkv(mem, p)
    for layer in range(w_in.shape[0]):
        x = _layer(x, kt, v, p, layer)
    return x
```

```python
import functools

import jax
import jax.numpy as jnp
import numpy as np
from jax import lax
from jax.experimental import pallas as pl
from jax.experimental.pallas import tpu as pltpu

D_MODEL = 1024
N_MEM = 256
HEAD_DIM = 64
N_Q_HEADS = 8
N_KV_HEADS = 2
Q_PER_KV = N_Q_HEADS // N_KV_HEADS
ATTN_WIDTH = N_Q_HEADS * HEAD_DIM
KV_WIDTH = N_KV_HEADS * HEAD_DIM
CONV_CH = D_MODEL - ATTN_WIDTH
CONV_K = 31
BLOCK = 128
N_X_HEADS = 4
X_HEAD_DIM = D_MODEL // N_X_HEADS
D_FF = 2816
EPS = 1e-6
NEG = -1e30

SUBLANES = 8
CONV_HALO = 32
CONV_ROWS = 32
LOG2E = 1.4426950408889634
TILE = 256
PIPELINE_LAG = 2
FFN_CHUNKS = ((0, 768), (768, 1536), (1536, 2304), (2304, D_FF))
VMEM_LIMIT_BYTES = 60 * 1024 * 1024
WEIGHT_DMAS_IN_FLIGHT = 4
MEMKV_BATCHES = 2

IN_Q = (0, ATTN_WIDTH)
IN_KV = (ATTN_WIDTH, ATTN_WIDTH + 2 * KV_WIDTH)
IN_VAL = (IN_KV[1], IN_KV[1] + CONV_CH)
IN_GATE = (IN_VAL[1], IN_VAL[1] + CONV_CH)

V_GMIX, V_GX, V_GFFN, V_GQ, V_GK, V_CB, V_LNG, V_LNB, V_GXQ, V_GMEM, V_GXK = range(11)
N_VEC_ROWS = 16

BF16 = jnp.bfloat16
F32 = jnp.float32


def _dot(a, b):
    return jnp.dot(a, b, preferred_element_type=F32)


def _dot_nt(a, b):
    return lax.dot_general(a, b, (((1,), (1,)), ((), ())), preferred_element_type=F32)


def _rms(x, g):
    return x * lax.rsqrt(jnp.mean(x * x, axis=-1, keepdims=True) + EPS) * g


def _sigmoid(x):
    return 1.0 / (1.0 + jnp.exp(-x))


def _vec(vec_ref, row, width=D_MODEL):
    return vec_ref[0, row:row + 1, :width]


def _cols(w_ref, span):
    return w_ref[:, span[0]:span[1]]


def _head_pair_rms(x, gain):
    lane = lax.broadcasted_iota(jnp.int32, (x.shape[0], 2 * HEAD_DIM), 1)
    first = lane < HEAD_DIM
    out = []
    for j in range(x.shape[1] // (2 * HEAD_DIM)):
        cols = slice(j * 2 * HEAD_DIM, (j + 1) * 2 * HEAD_DIM)
        blk = x[:, cols]
        sq = blk * blk
        ss_a = jnp.sum(jnp.where(first, sq, 0.0), axis=-1, keepdims=True)
        ss_b = jnp.sum(jnp.where(first, 0.0, sq), axis=-1, keepdims=True)
        inv = jnp.where(first, lax.rsqrt(ss_a * (1.0 / HEAD_DIM) + EPS), lax.rsqrt(ss_b * (1.0 / HEAD_DIM) + EPS))
        out.append(blk * inv * gain[:, cols])
    return out[0] if len(out) == 1 else jnp.concatenate(out, axis=1)


def _conv_prepare(h, w_in_ref, gsh_ref):
    tile = h.shape[0]
    val = _dot(h, _cols(w_in_ref, IN_VAL))
    gate = _dot(h, _cols(w_in_ref, IN_GATE))
    gsh_ref[0, CONV_HALO:CONV_HALO + tile, :] = val * _sigmoid(gate)
    n_rows = tile + CONV_HALO
    glu = gsh_ref[0].reshape(n_rows // SUBLANES, SUBLANES, CONV_CH)
    sub = lax.broadcasted_iota(jnp.int32, (n_rows // SUBLANES - 1, SUBLANES, CONV_CH), 1)
    for r in range(1, SUBLANES):
        merged = jnp.where(sub < r, glu[1:], glu[:-1])
        shifted = pltpu.roll(merged, SUBLANES - r, axis=1)
        gsh_ref[r, 0:n_rows - SUBLANES, :] = shifted.reshape(n_rows - SUBLANES, CONV_CH)


def _conv_rows(n0, gsh_ref, cw_ref, vec_ref, mixed_ref):
    groups = CONV_ROWS // SUBLANES
    y = jnp.broadcast_to(_vec(vec_ref, V_CB, CONV_CH), (groups, SUBLANES, CONV_CH))
    for j in range(CONV_K):
        a, r = divmod(j + CONV_HALO - (CONV_K - 1), SUBLANES)
        win = gsh_ref[r, n0 + SUBLANES * a:n0 + SUBLANES * a + CONV_ROWS, :]
        y = y + cw_ref[0, j] * win.reshape(groups, SUBLANES, CONV_CH)
    y = y.reshape(CONV_ROWS, CONV_CH)
    yc = y - jnp.mean(y, axis=-1, keepdims=True)
    yn = (yc * lax.rsqrt(jnp.mean(yc * yc, axis=-1, keepdims=True) + EPS) * _vec(vec_ref, V_LNG, CONV_CH)
          + _vec(vec_ref, V_LNB, CONV_CH))
    mixed_ref[n0:n0 + CONV_ROWS, ATTN_WIDTH:] = (yn * _sigmoid(yn)).astype(BF16)


def _swa_probs(i, qn, kn, v, first_of_seq, bias_ref, kprev_ref, vprev_ref):
    lane = lax.broadcasted_iota(jnp.int32, (2 * BLOCK, KV_WIDTH), 1)
    key = lax.broadcasted_iota(jnp.int32, (2 * BLOCK, KV_WIDTH), 0)
    head0 = (lane < HEAD_DIM) & (key > 0)
    head1 = (lane >= HEAD_DIM) & (key > 0)
    ind0 = jnp.where(lane < HEAD_DIM, 1.0, 0.0).astype(BF16)
    ind1 = jnp.where(lane < HEAD_DIM, 0.0, 1.0).astype(BF16)

    def masked(a, keep):
        return jnp.where(keep, a, 0.0).astype(BF16)

    rows = slice(i * BLOCK, (i + 1) * BLOCK)
    if i == 0:
        k_prev, v_prev = kprev_ref[...], vprev_ref[...]
        bias = bias_ref[0, jnp.where(first_of_seq, 1, 0)]
    else:
        prev = slice((i - 1) * BLOCK, i * BLOCK)
        k_prev, v_prev = kn[prev], v[prev]
        bias = bias_ref[0, 0]
    kk = jnp.concatenate([k_prev, kn[rows]], axis=0)
    vv = jnp.concatenate([v_prev, v[rows]], axis=0)
    k_st = jnp.concatenate([masked(kk, head0), masked(kk, head1)], axis=0)
    v_st = jnp.concatenate([jnp.concatenate([masked(vv, head0), ind0], axis=1),
                            jnp.concatenate([masked(vv, head1), ind1], axis=1)], axis=0)
    q_st = jnp.concatenate([qn[rows, j * KV_WIDTH:(j + 1) * KV_WIDTH] for j in range(Q_PER_KV)], axis=0)
    s = _dot_nt(q_st, k_st) + bias
    probs = []
    for c in range(N_KV_HEADS):
        sc = s[:, c * 2 * BLOCK:(c + 1) * 2 * BLOCK]
        probs.append(jnp.exp2(sc - jnp.max(sc, axis=-1, keepdims=True)).astype(BF16))
    return jnp.concatenate(probs, axis=1), v_st


def _swa_out(i, probs, v_st, mixed_ref):
    o = _dot(probs, v_st)
    o = o[:, :KV_WIDTH] / o[:, KV_WIDTH:]
    for j in range(Q_PER_KV):
        mixed_ref[i * BLOCK:(i + 1) * BLOCK, j * KV_WIDTH:(j + 1) * KV_WIDTH] = (
            o[j * BLOCK:(j + 1) * BLOCK].astype(BF16))


def _xattn_probs(hd, q, gq, kt_ref):
    cols = slice(hd * X_HEAD_DIM, (hd + 1) * X_HEAD_DIM)
    qn = _rms(q[:, cols], gq).astype(BF16)
    s = _dot(qn, kt_ref[cols, :])
    p = jnp.exp2(s - jnp.max(s, axis=-1, keepdims=True))
    return p.astype(BF16), 1.0 / jnp.sum(p, axis=-1, keepdims=True)


def _xattn_out(hd, p, inv, v_ref, att_ref):
    cols = slice(hd * X_HEAD_DIM, (hd + 1) * X_HEAD_DIM)
    att_ref[:, cols] = (_dot(p, v_ref[:, cols]) * inv).astype(BF16)


def _ffn_act(h, c0, c1, wgu_ref):
    gate = _dot(h, wgu_ref[:, c0:c1])
    up = _dot(h, wgu_ref[:, D_FF + c0:D_FF + c1])
    return (gate * _sigmoid(gate) * up).astype(BF16)


def _weight_chunks(layer, hbm, resident, stages):
    w_in_hbm, w_out_hbm, wqx_hbm, wox_hbm, wgu_hbm, wd_hbm = hbm
    w_in_ref, w_out_ref, wqx_ref, wox_ref, wgu_ref, wd_ref = resident
    used = [0] * len(stages)
    chunks = []

    def add(src, r0, c0, dst, d0, dc0, rows, cols, ring, pair_q_heads=False):
        slot = used[ring] % WEIGHT_DMAS_IN_FLIGHT
        used[ring] += 1
        assert rows <= stages[ring].shape[1] and cols <= stages[ring].shape[2]
        chunks.append((src.at[layer, pl.ds(r0, rows), pl.ds(c0, cols)],
                       stages[ring].at[slot, pl.ds(0, rows), pl.ds(0, cols)],
                       dst.at[pl.ds(d0, rows), pl.ds(dc0, cols)], pair_q_heads))

    wide, mid, narrow = range(3)
    rows_wide, rows_mid, rows_narrow = (stages[r].shape[1] for r in (wide, mid, narrow))
    for r0 in range(0, D_MODEL, rows_wide):
        add(wgu_hbm, r0, 0, wgu_ref, r0, 0, rows_wide, 2 * D_FF, wide)
    for r0 in range(0, D_FF, rows_narrow):
        add(wd_hbm, r0, 0, wd_ref, r0, 0, rows_narrow, D_MODEL, narrow)
    for r0 in range(0, D_MODEL, rows_narrow):
        add(w_in_hbm, r0, 0, w_in_ref, r0, 0, rows_narrow, ATTN_WIDTH, narrow, pair_q_heads=True)
    for r0 in range(0, D_MODEL, rows_mid):
        add(w_in_hbm, r0, ATTN_WIDTH, w_in_ref, r0, ATTN_WIDTH, rows_mid, IN_GATE[1] - ATTN_WIDTH, mid)
    for j in range(Q_PER_KV):
        for c in range(N_KV_HEADS):
            add(w_out_hbm, (j + Q_PER_KV * c) * HEAD_DIM, 0, w_out_ref, (N_KV_HEADS * j + c) * HEAD_DIM, 0,
                HEAD_DIM, D_MODEL, narrow)
    for r0 in range(ATTN_WIDTH, D_MODEL, rows_narrow):
        add(w_out_hbm, r0, 0, w_out_ref, r0, 0, rows_narrow, D_MODEL, narrow)
    for src, dst in ((wqx_hbm, wqx_ref), (wox_hbm, wox_ref)):
        for r0 in range(0, D_MODEL, rows_narrow):
            add(src, r0, 0, dst, r0, 0, rows_narrow, D_MODEL, narrow)
    return chunks


def _load_weights(chunks, sem):
    copies = [pltpu.make_async_copy(src, stage, sem.at[k % WEIGHT_DMAS_IN_FLIGHT])
              for k, (src, stage, _, _) in enumerate(chunks)]
    for copy in copies[:WEIGHT_DMAS_IN_FLIGHT]:
        copy.start()
    for k, (_, stage, dst, pair_q_heads) in enumerate(chunks):
        copies[k].wait()
        w = stage[...]
        if pair_q_heads:
            w = jnp.concatenate([w[:, (j + Q_PER_KV * c) * HEAD_DIM:(j + Q_PER_KV * c + 1) * HEAD_DIM]
                                 for j in range(Q_PER_KV) for c in range(N_KV_HEADS)], axis=1)
        dst[...] = w.astype(BF16)
        if k + WEIGHT_DMAS_IN_FLIGHT < len(copies):
            copies[k + WEIGHT_DMAS_IN_FLIGHT].start()


def _layer_kernel(layer, tiles_per_seq,
                  x_ref, kt_ref, v_ref, vec_ref, bias_ref, cw_ref,
                  w_in_hbm, w_out_hbm, wqx_hbm, wox_hbm, wgu_hbm, wd_hbm,
                  o_ref,
                  kprev_ref, vprev_ref, gsh_ref, mixed_ref, att_ref, x1_ref, x2_ref,
                  w_in_ref, w_out_ref, wqx_ref, wox_ref, wgu_ref, wd_ref,
                  stage_wide_ref, stage_mid_ref, stage_narrow_ref, weight_sem):
    step = pl.program_id(0)
    first_of_seq = step % tiles_per_seq == 0

    @pl.when(step == 0)
    def _():
        x1_ref[...] = jnp.zeros_like(x1_ref)
        x2_ref[...] = jnp.zeros_like(x2_ref)
        _load_weights(_weight_chunks(layer, (w_in_hbm, w_out_hbm, wqx_hbm, wox_hbm, wgu_hbm, wd_hbm),
                                     (w_in_ref, w_out_ref, wqx_ref, wox_ref, wgu_ref, wd_ref),
                                     (stage_wide_ref, stage_mid_ref, stage_narrow_ref)), weight_sem)

    @pl.when(first_of_seq)
    def _():
        kprev_ref[...] = jnp.zeros_like(kprev_ref)
        vprev_ref[...] = jnp.zeros_like(vprev_ref)
        gsh_ref[0, 0:CONV_HALO, :] = jnp.zeros((CONV_HALO, CONV_CH), F32)

    conv = functools.partial(_conv_rows, gsh_ref=gsh_ref, cw_ref=cw_ref, vec_ref=vec_ref, mixed_ref=mixed_ref)
    conv_starts = list(range(0, TILE, CONV_ROWS))

    def ffn_act(c):
        return _ffn_act(h_ffn, *FFN_CHUNKS[c], wgu_ref)

    def ffn_down(c, act):
        c0, c1 = FFN_CHUNKS[c]
        return _dot(act, wd_ref[c0:c1, :])

    h_mix = _rms(x_ref[...], _vec(vec_ref, V_GMIX)).astype(BF16)
    _conv_prepare(h_mix, w_in_ref, gsh_ref)
    h_ffn = _rms(x2_ref[...], _vec(vec_ref, V_GFFN)).astype(BF16)
    act = ffn_act(0)
    for n0 in conv_starts[0:3]:
        conv(n0)
    h_x = _rms(x1_ref[...], _vec(vec_ref, V_GX)).astype(BF16)
    qn = _head_pair_rms(_dot(h_mix, _cols(w_in_ref, IN_Q)), _vec(vec_ref, V_GQ, ATTN_WIDTH)).astype(BF16)
    kv = _dot(h_mix, _cols(w_in_ref, IN_KV))
    kn = _head_pair_rms(kv[:, :KV_WIDTH], _vec(vec_ref, V_GK, KV_WIDTH))
    v = kv[:, KV_WIDTH:]
    q_x = _dot(h_x, wqx_ref[...])
    swa_probs = functools.partial(_swa_probs, qn=qn, kn=kn, v=v, first_of_seq=first_of_seq, bias_ref=bias_ref,
                                  kprev_ref=kprev_ref, vprev_ref=vprev_ref)
    xattn_probs = functools.partial(_xattn_probs, q=q_x, gq=_vec(vec_ref, V_GXQ, X_HEAD_DIM), kt_ref=kt_ref)
    y = ffn_down(0, act)
    act = ffn_act(1)
    for n0 in conv_starts[3:6]:
        conv(n0)
    swa0 = swa_probs(0)
    xp = [xattn_probs(0), xattn_probs(1)]
    y = y + ffn_down(1, act)
    act = ffn_act(2)
    for n0 in conv_starts[6:8]:
        conv(n0)
    gsh_ref[0, 0:CONV_HALO, :] = gsh_ref[0, TILE:TILE + CONV_HALO, :]
    _swa_out(0, *swa0, mixed_ref)
    for hd in (0, 1):
        _xattn_out(hd, *xp[hd], v_ref, att_ref)
    swa1 = swa_probs(1)
    xp = [xattn_probs(2), xattn_probs(3)]
    y = y + ffn_down(2, act)
    act = ffn_act(3)
    _swa_out(1, *swa1, mixed_ref)
    for hd in (2, 3):
        _xattn_out(hd, *xp[hd - 2], v_ref, att_ref)
    x2_new = x1_ref[...] + _dot(att_ref[...], wox_ref[...])
    y = y + ffn_down(3, act)
    x1_new = x_ref[...] + _dot(mixed_ref[...], w_out_ref[...])
    kprev_ref[...] = kn[TILE - BLOCK:]
    vprev_ref[...] = v[TILE - BLOCK:]
    o_ref[...] = x2_ref[...] + y
    x2_ref[...] = x2_new
    x1_ref[...] = x1_new


def _layer_spec(a, layer, **kwargs):
    return pl.BlockSpec((1,) + a.shape[1:], lambda i: (layer,) + (0,) * (a.ndim - 1), **kwargs)


def _layer(x, kt, v, p, layer):
    batch, seq, _ = x.shape
    tiles_per_seq = seq // TILE
    n_tiles = batch * tiles_per_seq
    xf = x.reshape(batch * seq, D_MODEL)

    def mem_batch(i):
        return jnp.clip(i - 1, 0, n_tiles - 1) // tiles_per_seq

    consts = [p["vec"], p["bias"], p["cw"]]
    weights = [p["w_in"], p["w_out"], p["wq_x"], p["wo_x"], p["w_gate_up"], p["w_down"]]
    out = pl.pallas_call(
        functools.partial(_layer_kernel, layer, tiles_per_seq),
        grid=(n_tiles + PIPELINE_LAG,),
        in_specs=[pl.BlockSpec((TILE, D_MODEL), lambda i: (jnp.minimum(i, n_tiles - 1), 0)),
                  pl.BlockSpec((None, None, D_MODEL, N_MEM), lambda i: (layer, mem_batch(i), 0, 0)),
                  pl.BlockSpec((None, None, N_MEM, D_MODEL), lambda i: (layer, mem_batch(i), 0, 0))]
                 + [_layer_spec(c, layer, pipeline_mode=pl.Buffered(1)) for c in consts]
                 + [pl.BlockSpec(memory_space=pl.ANY)] * len(weights),
        out_specs=pl.BlockSpec((TILE, D_MODEL), lambda i: (jnp.maximum(i - PIPELINE_LAG, 0), 0)),
        out_shape=jax.ShapeDtypeStruct(xf.shape, F32),
        scratch_shapes=[
            pltpu.VMEM((BLOCK, KV_WIDTH), F32),
            pltpu.VMEM((BLOCK, KV_WIDTH), F32),
            pltpu.VMEM((SUBLANES, TILE + CONV_HALO, CONV_CH), F32),
            pltpu.VMEM((TILE, D_MODEL), BF16),
            pltpu.VMEM((TILE, D_MODEL), BF16),
            pltpu.VMEM((TILE, D_MODEL), F32),
            pltpu.VMEM((TILE, D_MODEL), F32),
            pltpu.VMEM((D_MODEL, IN_GATE[1]), BF16),
            pltpu.VMEM((D_MODEL, D_MODEL), BF16),
            pltpu.VMEM((D_MODEL, D_MODEL), BF16),
            pltpu.VMEM((D_MODEL, D_MODEL), BF16),
            pltpu.VMEM((D_MODEL, 2 * D_FF), BF16),
            pltpu.VMEM((D_FF, D_MODEL), BF16),
            pltpu.VMEM((WEIGHT_DMAS_IN_FLIGHT, 64, 2 * D_FF), F32),
            pltpu.VMEM((WEIGHT_DMAS_IN_FLIGHT, 128, IN_GATE[1] - ATTN_WIDTH), F32),
            pltpu.VMEM((WEIGHT_DMAS_IN_FLIGHT, 256, D_MODEL), F32),
            pltpu.SemaphoreType.DMA((WEIGHT_DMAS_IN_FLIGHT,)),
        ],
        compiler_params=pltpu.CompilerParams(
            dimension_semantics=("arbitrary",), vmem_limit_bytes=VMEM_LIMIT_BYTES),
        name="layer",
    )(xf, kt, v, *consts, *weights)
    return out.reshape(x.shape)


def _memkv_kernel(mem_ref, vec_ref, wkv_ref, kt_ref, v_ref):
    n = mem_ref.shape[0]
    mem_n = _rms(mem_ref[...].reshape(n * N_MEM, D_MODEL), _vec(vec_ref, V_GMEM)).astype(BF16)
    kv = _dot(mem_n, wkv_ref[0].astype(BF16))
    gk = _vec(vec_ref, V_GXK, X_HEAD_DIM)
    for b in range(n):
        rows = slice(b * N_MEM, (b + 1) * N_MEM)
        for hd in range(N_X_HEADS):
            cols = slice(hd * X_HEAD_DIM, (hd + 1) * X_HEAD_DIM)
            kt_ref[0, b, cols, :] = _rms(kv[rows, cols], gk).T.astype(BF16)
        v_ref[0, b] = kv[rows, D_MODEL:].astype(BF16)


def _memkv(mem, p):
    batch = mem.shape[0]
    n_layers = p["wkv_x"].shape[0]

    def by_layer(a):
        return pl.BlockSpec((1,) + a.shape[1:], lambda l, j: (l,) + (0,) * (a.ndim - 1))

    return pl.pallas_call(
        _memkv_kernel,
        grid=(n_layers, batch // MEMKV_BATCHES),
        in_specs=[pl.BlockSpec((MEMKV_BATCHES, N_MEM, D_MODEL), lambda l, j: (j, 0, 0)),
                  by_layer(p["vec"]), by_layer(p["wkv_x"])],
        out_specs=[pl.BlockSpec((1, MEMKV_BATCHES, D_MODEL, N_MEM), lambda l, j: (l, j, 0, 0)),
                   pl.BlockSpec((1, MEMKV_BATCHES, N_MEM, D_MODEL), lambda l, j: (l, j, 0, 0))],
        out_shape=[jax.ShapeDtypeStruct((n_layers, batch, D_MODEL, N_MEM), BF16),
                   jax.ShapeDtypeStruct((n_layers, batch, N_MEM, D_MODEL), BF16)],
        compiler_params=pltpu.CompilerParams(
            dimension_semantics=("arbitrary", "arbitrary"), vmem_limit_bytes=VMEM_LIMIT_BYTES),
        name="memkv",
    )(mem, p["vec"], p["wkv_x"])


def _swa_bias(sinks):
    qi = np.arange(BLOCK)[:, None]
    sk = np.arange(2 * BLOCK)[None, :]
    dist = qi + BLOCK - sk
    valid = (dist >= 0) & (dist < BLOCK)
    assert not valid[:, 0].any()
    slopes = 2.0 ** (-8.0 * (np.arange(N_Q_HEADS, dtype=np.float32) + 1.0) / N_Q_HEADS)
    table = np.empty((2, Q_PER_KV * BLOCK, N_KV_HEADS * 2 * BLOCK), np.float32)
    for first in range(2):
        ok = valid & (sk >= BLOCK) if first else valid
        for j in range(Q_PER_KV):
            for c in range(N_KV_HEADS):
                slope = slopes[j + Q_PER_KV * c]
                blk = np.where(ok, -slope * LOG2E * dist.astype(np.float32), np.float32(NEG))
                table[first, j * BLOCK:(j + 1) * BLOCK, c * 2 * BLOCK:(c + 1) * 2 * BLOCK] = blk
    sink_col = np.zeros(table.shape[-1], bool)
    sink_col[::2 * BLOCK] = True
    n_layers = sinks.shape[0]
    logit = sinks.astype(F32).reshape(n_layers, N_KV_HEADS, Q_PER_KV) * LOG2E
    logit = jnp.repeat(jnp.repeat(jnp.swapaxes(logit, 1, 2), BLOCK, axis=1), 2 * BLOCK, axis=2)
    return jnp.where(jnp.asarray(sink_col), logit[:, None], jnp.asarray(table)[None])


def _prepare_params(norm_mix_g, w_in, q_norm_g, k_norm_g, sinks, conv_w, conv_b, conv_ln_g, conv_ln_b, w_out,
                    norm_x_g, norm_mem_g, wq_x, wkv_x, xq_norm_g, xk_norm_g, wo_x, norm_ffn_g, w_gate_up, w_down):
    n_layers = w_in.shape[0]

    def padded(a):
        a = a.astype(F32)
        return jnp.pad(a, ((0, 0), (0, D_MODEL - a.shape[1])))

    rows = [None] * N_VEC_ROWS
    rows[V_GMIX] = padded(norm_mix_g)
    rows[V_GX] = padded(norm_x_g)
    rows[V_GFFN] = padded(norm_ffn_g)
    rows[V_GQ] = padded(jnp.tile(q_norm_g, (1, N_Q_HEADS)) * (HEAD_DIM ** -0.5 * LOG2E))
    rows[V_GK] = padded(jnp.tile(k_norm_g, (1, N_KV_HEADS)))
    rows[V_CB] = padded(conv_b)
    rows[V_LNG] = padded(conv_ln_g)
    rows[V_LNB] = padded(conv_ln_b)
    rows[V_GXQ] = padded(xq_norm_g * (X_HEAD_DIM ** -0.5 * LOG2E))
    rows[V_GMEM] = padded(norm_mem_g)
    rows[V_GXK] = padded(xk_norm_g)
    zero_row = jnp.zeros((n_layers, D_MODEL), F32)
    vec = jnp.stack([zero_row if r is None else r for r in rows], axis=1)

    return dict(
        vec=vec,
        bias=_swa_bias(sinks),
        cw=jnp.broadcast_to(conv_w.astype(F32)[:, :, None, :], (n_layers, CONV_K, SUBLANES, CONV_CH)),
        w_in=w_in, w_out=w_out, wq_x=wq_x, wkv_x=wkv_x, wo_x=wo_x, w_gate_up=w_gate_up, w_down=w_down,
    )


def kernel(x, mem, norm_mix_g, w_in, q_norm_g, k_norm_g, sinks, conv_w, conv_b, conv_ln_g, conv_ln_b, w_out, norm_x_g, norm_mem_g, wq_x, wkv_x, xq_norm_g, xk_norm_g, wo_x, norm_ffn_g, w_gate_up, w_down):
    p = _prepare_params(norm_mix_g, w_in, q_norm_g, k_norm_g, sinks, conv_w, conv_b, conv_ln_g, conv_ln_b, w_out,
                        norm_x_g, norm_mem_g, wq_x, wkv_x, xq_norm_g, xk_norm_g, wo_x, norm_ffn_g, w_gate_up,
                        w_down)
    kt, v = _memkv(mem, p)
    for layer in range(w_in.shape[0]):
        x = _layer(x, kt, v, p, layer)
    return x
```

```python
import functools

import jax
import jax.numpy as jnp
import numpy as np
from jax import lax
from jax.experimental import pallas as pl
from jax.experimental.pallas import tpu as pltpu

D_MODEL = 1024
N_MEM = 256
HEAD_DIM = 64
N_Q_HEADS = 8
N_KV_HEADS = 2
Q_PER_KV = N_Q_HEADS // N_KV_HEADS
ATTN_WIDTH = N_Q_HEADS * HEAD_DIM
KV_WIDTH = N_KV_HEADS * HEAD_DIM
CONV_CH = D_MODEL - ATTN_WIDTH
CONV_K = 31
BLOCK = 128
N_X_HEADS = 4
X_HEAD_DIM = D_MODEL // N_X_HEADS
D_FF = 2816
EPS = 1e-6
NEG = -1e30

SUBLANES = 8
CONV_HALO = 32
CONV_ROWS = 32
LOG2E = 1.4426950408889634
TILE = 256
PIPELINE_LAG = 2
FFN_CHUNKS = ((0, 768), (768, 1536), (1536, 2304), (2304, D_FF))
VMEM_LIMIT_BYTES = 60 * 1024 * 1024
WEIGHT_DMAS_IN_FLIGHT = 4
MEMKV_BATCHES = 2

IN_Q = (0, ATTN_WIDTH)
IN_KV = (ATTN_WIDTH, ATTN_WIDTH + 2 * KV_WIDTH)
IN_VAL = (IN_KV[1], IN_KV[1] + CONV_CH)
IN_GATE = (IN_VAL[1], IN_VAL[1] + CONV_CH)

V_GMIX, V_GX, V_GFFN, V_GQ, V_GK, V_CB, V_LNG, V_LNB, V_GXQ, V_GMEM, V_GXK = range(11)
N_VEC_ROWS = 16

BF16 = jnp.bfloat16
F32 = jnp.float32


def _dot(a, b):
    return jnp.dot(a, b, preferred_element_type=F32)


def _dot_nt(a, b):
    return lax.dot_general(a, b, (((1,), (1,)), ((), ())), preferred_element_type=F32)


def _rms(x, g):
    return x * lax.rsqrt(jnp.mean(x * x, axis=-1, keepdims=True) + EPS) * g


def _sigmoid(x):
    return 1.0 / (1.0 + jnp.exp(-x))


def _vec(vec_ref, row, width=D_MODEL):
    return vec_ref[0, row:row + 1, :width]


def _cols(w_ref, span):
    return w_ref[:, span[0]:span[1]]


def _head_pair_rms(x, gain):
    lane = lax.broadcasted_iota(jnp.int32, (x.shape[0], 2 * HEAD_DIM), 1)
    first = lane < HEAD_DIM
    out = []
    for j in range(x.shape[1] // (2 * HEAD_DIM)):
        cols = slice(j * 2 * HEAD_DIM, (j + 1) * 2 * HEAD_DIM)
        blk = x[:, cols]
        sq = blk * blk
        ss_a = jnp.sum(jnp.where(first, sq, 0.0), axis=-1, keepdims=True)
        ss_b = jnp.sum(jnp.where(first, 0.0, sq), axis=-1, keepdims=True)
        inv = jnp.where(first, lax.rsqrt(ss_a * (1.0 / HEAD_DIM) + EPS), lax.rsqrt(ss_b * (1.0 / HEAD_DIM) + EPS))
        out.append(blk * inv * gain[:, cols])
    return out[0] if len(out) == 1 else jnp.concatenate(out, axis=1)


def _conv_prepare(h, w_in_ref, gsh_ref):
    tile = h.shape[0]
    val = _dot(h, _cols(w_in_ref, IN_VAL))
    gate = _dot(h, _cols(w_in_ref, IN_GATE))
    gsh_ref[0, CONV_HALO:CONV_HALO + tile, :] = val * _sigmoid(gate)
    n_rows = tile + CONV_HALO
    glu = gsh_ref[0].reshape(n_rows // SUBLANES, SUBLANES, CONV_CH)
    sub = lax.broadcasted_iota(jnp.int32, (n_rows // SUBLANES - 1, SUBLANES, CONV_CH), 1)
    for r in range(1, SUBLANES):
        merged = jnp.where(sub < r, glu[1:], glu[:-1])
        shifted = pltpu.roll(merged, SUBLANES - r, axis=1)
        gsh_ref[r, 0:n_rows - SUBLANES, :] = shifted.reshape(n_rows - SUBLANES, CONV_CH)


def _conv_rows(n0, gsh_ref, cw_ref, vec_ref, mixed_ref):
    groups = CONV_ROWS // SUBLANES
    y = jnp.broadcast_to(_vec(vec_ref, V_CB, CONV_CH), (groups, SUBLANES, CONV_CH))
    for j in range(CONV_K):
        a, r = divmod(j + CONV_HALO - (CONV_K - 1), SUBLANES)
        win = gsh_ref[r, n0 + SUBLANES * a:n0 + SUBLANES * a + CONV_ROWS, :]
        y = y + cw_ref[0, j] * win.reshape(groups, SUBLANES, CONV_CH)
    y = y.reshape(CONV_ROWS, CONV_CH)
    yc = y - jnp.mean(y, axis=-1, keepdims=True)
    yn = (yc * lax.rsqrt(jnp.mean(yc * yc, axis=-1, keepdims=True) + EPS) * _vec(vec_ref, V_LNG, CONV_CH)
          + _vec(vec_ref, V_LNB, CONV_CH))
    mixed_ref[n0:n0 + CONV_ROWS, ATTN_WIDTH:] = (yn * _sigmoid(yn)).astype(BF16)


def _swa_probs(i, qn, kn, v, first_of_seq, bias_ref, kprev_ref, vprev_ref):
    lane = lax.broadcasted_iota(jnp.int32, (2 * BLOCK, KV_WIDTH), 1)
    key = lax.broadcasted_iota(jnp.int32, (2 * BLOCK, KV_WIDTH), 0)
    head0 = (lane < HEAD_DIM) & (key > 0)
    head1 = (lane >= HEAD_DIM) & (key > 0)
    ind0 = jnp.where(lane < HEAD_DIM, 1.0, 0.0).astype(BF16)
    ind1 = jnp.where(lane < HEAD_DIM, 0.0, 1.0).astype(BF16)

    def masked(a, keep):
        return jnp.where(keep, a, 0.0).astype(BF16)

    rows = slice(i * BLOCK, (i + 1) * BLOCK)
    if i == 0:
        k_prev, v_prev = kprev_ref[...], vprev_ref[...]
        bias = bias_ref[0, jnp.where(first_of_seq, 1, 0)]
    else:
        prev = slice((i - 1) * BLOCK, i * BLOCK)
        k_prev, v_prev = kn[prev], v[prev]
        bias = bias_ref[0, 0]
    kk = jnp.concatenate([k_prev, kn[rows]], axis=0)
    vv = jnp.concatenate([v_prev, v[rows]], axis=0)
    k_st = jnp.concatenate([masked(kk, head0), masked(kk, head1)], axis=0)
    v_st = jnp.concatenate([jnp.concatenate([masked(vv, head0), ind0], axis=1),
                            jnp.concatenate([masked(vv, head1), ind1], axis=1)], axis=0)
    q_st = jnp.concatenate([qn[rows, j * KV_WIDTH:(j + 1) * KV_WIDTH] for j in range(Q_PER_KV)], axis=0)
    s = _dot_nt(q_st, k_st) + bias
    probs = []
    for c in range(N_KV_HEADS):
        sc = s[:, c * 2 * BLOCK:(c + 1) * 2 * BLOCK]
        probs.append(jnp.exp2(sc - jnp.max(sc, axis=-1, keepdims=True)).astype(BF16))
    return jnp.concatenate(probs, axis=1), v_st


def _swa_out(i, probs, v_st, mixed_ref):
    o = _dot(probs, v_st)
    o = o[:, :KV_WIDTH] / o[:, KV_WIDTH:]
    for j in range(Q_PER_KV):
        mixed_ref[i * BLOCK:(i + 1) * BLOCK, j * KV_WIDTH:(j + 1) * KV_WIDTH] = (
            o[j * BLOCK:(j + 1) * BLOCK].astype(BF16))


def _xattn_probs(hd, q, gq, kt_ref):
    cols = slice(hd * X_HEAD_DIM, (hd + 1) * X_HEAD_DIM)
    qn = _rms(q[:, cols], gq).astype(BF16)
    s = _dot(qn, kt_ref[cols, :])
    p = jnp.exp2(s - jnp.max(s, axis=-1, keepdims=True))
    return p.astype(BF16), 1.0 / jnp.sum(p, axis=-1, keepdims=True)


def _xattn_out(hd, p, inv, v_ref, att_ref):
    cols = slice(hd * X_HEAD_DIM, (hd + 1) * X_HEAD_DIM)
    att_ref[:, cols] = (_dot(p, v_ref[:, cols]) * inv).astype(BF16)


def _ffn_act(h, c0, c1, wgu_ref):
    gate = _dot(h, wgu_ref[:, c0:c1])
    up = _dot(h, wgu_ref[:, D_FF + c0:D_FF + c1])
    return (gate * _sigmoid(gate) * up).astype(BF16)


def _weight_chunks(layer, hbm, resident, stages):
    w_in_hbm, w_out_hbm, wqx_hbm, wox_hbm, wgu_hbm, wd_hbm = hbm
    w_in_ref, w_out_ref, wqx_ref, wox_ref, wgu_ref, wd_ref = resident
    used = [0] * len(stages)
    chunks = []

    def add(src, r0, c0, dst, d0, dc0, rows, cols, ring, pair_q_heads=False):
        slot = used[ring] % WEIGHT_DMAS_IN_FLIGHT
        used[ring] += 1
        assert rows <= stages[ring].shape[1] and cols <= stages[ring].shape[2]
        chunks.append((src.at[layer, pl.ds(r0, rows), pl.ds(c0, cols)],
                       stages[ring].at[slot, pl.ds(0, rows), pl.ds(0, cols)],
                       dst.at[pl.ds(d0, rows), pl.ds(dc0, cols)], pair_q_heads))

    wide, mid, narrow = range(3)
    rows_wide, rows_mid, rows_narrow = (stages[r].shape[1] for r in (wide, mid, narrow))
    for r0 in range(0, D_MODEL, rows_wide):
        add(wgu_hbm, r0, 0, wgu_ref, r0, 0, rows_wide, 2 * D_FF, wide)
    for r0 in range(0, D_FF, rows_narrow):
        add(wd_hbm, r0, 0, wd_ref, r0, 0, rows_narrow, D_MODEL, narrow)
    for r0 in range(0, D_MODEL, rows_narrow):
        add(w_in_hbm, r0, 0, w_in_ref, r0, 0, rows_narrow, ATTN_WIDTH, narrow, pair_q_heads=True)
    for r0 in range(0, D_MODEL, rows_mid):
        add(w_in_hbm, r0, ATTN_WIDTH, w_in_ref, r0, ATTN_WIDTH, rows_mid, IN_GATE[1] - ATTN_WIDTH, mid)
    for j in range(Q_PER_KV):
        for c in range(N_KV_HEADS):
            add(w_out_hbm, (j + Q_PER_KV * c) * HEAD_DIM, 0, w_out_ref, (N_KV_HEADS * j + c) * HEAD_DIM, 0,
                HEAD_DIM, D_MODEL, narrow)
    for r0 in range(ATTN_WIDTH, D_MODEL, rows_narrow):
        add(w_out_hbm, r0, 0, w_out_ref, r0, 0, rows_narrow, D_MODEL, narrow)
    for src, dst in ((wqx_hbm, wqx_ref), (wox_hbm, wox_ref)):
        for r0 in range(0, D_MODEL, rows_narrow):
            add(src, r0, 0, dst, r0, 0, rows_narrow, D_MODEL, narrow)
    return chunks


def _load_weights(chunks, sem):
    copies = [pltpu.make_async_copy(src, stage, sem.at[k % WEIGHT_DMAS_IN_FLIGHT])
              for k, (src, stage, _, _) in enumerate(chunks)]
    for k, copy in enumerate(copies[:WEIGHT_DMAS_IN_FLIGHT]):
        copy.start(priority=k % 2)
    for k, (_, stage, dst, pair_q_heads) in enumerate(chunks):
        copies[k].wait()
        w = stage[...]
        if pair_q_heads:
            w = jnp.concatenate([w[:, (j + Q_PER_KV * c) * HEAD_DIM:(j + Q_PER_KV * c + 1) * HEAD_DIM]
                                 for j in range(Q_PER_KV) for c in range(N_KV_HEADS)], axis=1)
        dst[...] = w.astype(BF16)
        if k + WEIGHT_DMAS_IN_FLIGHT < len(copies):
            copies[k + WEIGHT_DMAS_IN_FLIGHT].start(priority=(k + WEIGHT_DMAS_IN_FLIGHT) % 2)


def _layer_kernel(layer, tiles_per_seq,
                  x_ref, kt_ref, v_ref, vec_ref, bias_ref, cw_ref,
                  w_in_hbm, w_out_hbm, wqx_hbm, wox_hbm, wgu_hbm, wd_hbm,
                  o_ref,
                  kprev_ref, vprev_ref, gsh_ref, mixed_ref, att_ref, x1_ref, x2_ref,
                  w_in_ref, w_out_ref, wqx_ref, wox_ref, wgu_ref, wd_ref,
                  stage_wide_ref, stage_mid_ref, stage_narrow_ref, weight_sem):
    step = pl.program_id(0)
    first_of_seq = step % tiles_per_seq == 0

    @pl.when(step == 0)
    def _():
        x1_ref[...] = jnp.zeros_like(x1_ref)
        x2_ref[...] = jnp.zeros_like(x2_ref)
        _load_weights(_weight_chunks(layer, (w_in_hbm, w_out_hbm, wqx_hbm, wox_hbm, wgu_hbm, wd_hbm),
                                     (w_in_ref, w_out_ref, wqx_ref, wox_ref, wgu_ref, wd_ref),
                                     (stage_wide_ref, stage_mid_ref, stage_narrow_ref)), weight_sem)

    @pl.when(first_of_seq)
    def _():
        kprev_ref[...] = jnp.zeros_like(kprev_ref)
        vprev_ref[...] = jnp.zeros_like(vprev_ref)
        gsh_ref[0, 0:CONV_HALO, :] = jnp.zeros((CONV_HALO, CONV_CH), F32)

    conv = functools.partial(_conv_rows, gsh_ref=gsh_ref, cw_ref=cw_ref, vec_ref=vec_ref, mixed_ref=mixed_ref)
    conv_starts = list(range(0, TILE, CONV_ROWS))

    def ffn_act(c):
        return _ffn_act(h_ffn, *FFN_CHUNKS[c], wgu_ref)

    def ffn_down(c, act):
        c0, c1 = FFN_CHUNKS[c]
        return _dot(act, wd_ref[c0:c1, :])

    h_mix = _rms(x_ref[...], _vec(vec_ref, V_GMIX)).astype(BF16)
    _conv_prepare(h_mix, w_in_ref, gsh_ref)
    h_ffn = _rms(x2_ref[...], _vec(vec_ref, V_GFFN)).astype(BF16)
    act = ffn_act(0)
    for n0 in conv_starts[0:3]:
        conv(n0)
    h_x = _rms(x1_ref[...], _vec(vec_ref, V_GX)).astype(BF16)
    qn = _head_pair_rms(_dot(h_mix, _cols(w_in_ref, IN_Q)), _vec(vec_ref, V_GQ, ATTN_WIDTH)).astype(BF16)
    kv = _dot(h_mix, _cols(w_in_ref, IN_KV))
    kn = _head_pair_rms(kv[:, :KV_WIDTH], _vec(vec_ref, V_GK, KV_WIDTH))
    v = kv[:, KV_WIDTH:]
    q_x = _dot(h_x, wqx_ref[...])
    swa_probs = functools.partial(_swa_probs, qn=qn, kn=kn, v=v, first_of_seq=first_of_seq, bias_ref=bias_ref,
                                  kprev_ref=kprev_ref, vprev_ref=vprev_ref)
    xattn_probs = functools.partial(_xattn_probs, q=q_x, gq=_vec(vec_ref, V_GXQ, X_HEAD_DIM), kt_ref=kt_ref)
    y = ffn_down(0, act)
    act = ffn_act(1)
    for n0 in conv_starts[3:6]:
        conv(n0)
    swa0 = swa_probs(0)
    xp = [xattn_probs(0), xattn_probs(1)]
    y = y + ffn_down(1, act)
    act = ffn_act(2)
    for n0 in conv_starts[6:8]:
        conv(n0)
    gsh_ref[0, 0:CONV_HALO, :] = gsh_ref[0, TILE:TILE + CONV_HALO, :]
    _swa_out(0, *swa0, mixed_ref)
    for hd in (0, 1):
        _xattn_out(hd, *xp[hd], v_ref, att_ref)
    swa1 = swa_probs(1)
    xp = [xattn_probs(2), xattn_probs(3)]
    y = y + ffn_down(2, act)
    act = ffn_act(3)
    _swa_out(1, *swa1, mixed_ref)
    for hd in (2, 3):
        _xattn_out(hd, *xp[hd - 2], v_ref, att_ref)
    x2_new = x1_ref[...] + _dot(att_ref[...], wox_ref[...])
    y = y + ffn_down(3, act)
    x1_new = x_ref[...] + _dot(mixed_ref[...], w_out_ref[...])
    kprev_ref[...] = kn[TILE - BLOCK:]
    vprev_ref[...] = v[TILE - BLOCK:]
    o_ref[...] = x2_ref[...] + y
    x2_ref[...] = x2_new
    x1_ref[...] = x1_new


def _layer_spec(a, layer, **kwargs):
    return pl.BlockSpec((1,) + a.shape[1:], lambda i: (layer,) + (0,) * (a.ndim - 1), **kwargs)


def _layer(x, kt, v, p, layer):
    batch, seq, _ = x.shape
    tiles_per_seq = seq // TILE
    n_tiles = batch * tiles_per_seq
    xf = x.reshape(batch * seq, D_MODEL)

    def mem_batch(i):
        return jnp.clip(i - 1, 0, n_tiles - 1) // tiles_per_seq

    consts = [p["vec"], p["bias"], p["cw"]]
    weights = [p["w_in"], p["w_out"], p["wq_x"], p["wo_x"], p["w_gate_up"], p["w_down"]]
    out = pl.pallas_call(
        functools.partial(_layer_kernel, layer, tiles_per_seq),
        grid=(n_tiles + PIPELINE_LAG,),
        in_specs=[pl.BlockSpec((TILE, D_MODEL), lambda i: (jnp.minimum(i, n_tiles - 1), 0)),
                  pl.BlockSpec((None, None, D_MODEL, N_MEM), lambda i: (layer, mem_batch(i), 0, 0)),
                  pl.BlockSpec((None, None, N_MEM, D_MODEL), lambda i: (layer, mem_batch(i), 0, 0))]
                 + [_layer_spec(c, layer, pipeline_mode=pl.Buffered(1)) for c in consts]
                 + [pl.BlockSpec(memory_space=pl.ANY)] * len(weights),
        out_specs=pl.BlockSpec((TILE, D_MODEL), lambda i: (jnp.maximum(i - PIPELINE_LAG, 0), 0)),
        out_shape=jax.ShapeDtypeStruct(xf.shape, F32),
        scratch_shapes=[
            pltpu.VMEM((BLOCK, KV_WIDTH), F32),
            pltpu.VMEM((BLOCK, KV_WIDTH), F32),
            pltpu.VMEM((SUBLANES, TILE + CONV_HALO, CONV_CH), F32),
            pltpu.VMEM((TILE, D_MODEL), BF16),
            pltpu.VMEM((TILE, D_MODEL), BF16),
            pltpu.VMEM((TILE, D_MODEL), F32),
            pltpu.VMEM((TILE, D_MODEL), F32),
            pltpu.VMEM((D_MODEL, IN_GATE[1]), BF16),
            pltpu.VMEM((D_MODEL, D_MODEL), BF16),
            pltpu.VMEM((D_MODEL, D_MODEL), BF16),
            pltpu.VMEM((D_MODEL, D_MODEL), BF16),
            pltpu.VMEM((D_MODEL, 2 * D_FF), BF16),
            pltpu.VMEM((D_FF, D_MODEL), BF16),
            pltpu.VMEM((WEIGHT_DMAS_IN_FLIGHT, 64, 2 * D_FF), F32),
            pltpu.VMEM((WEIGHT_DMAS_IN_FLIGHT, 128, IN_GATE[1] - ATTN_WIDTH), F32),
            pltpu.VMEM((WEIGHT_DMAS_IN_FLIGHT, 256, D_MODEL), F32),
            pltpu.SemaphoreType.DMA((WEIGHT_DMAS_IN_FLIGHT,)),
        ],
        compiler_params=pltpu.CompilerParams(
            dimension_semantics=("arbitrary",), vmem_limit_bytes=VMEM_LIMIT_BYTES),
        name="layer",
    )(xf, kt, v, *consts, *weights)
    return out.reshape(x.shape)


def _memkv_kernel(mem_ref, vec_ref, wkv_ref, kt_ref, v_ref):
    n = mem_ref.shape[0]
    mem_n = _rms(mem_ref[...].reshape(n * N_MEM, D_MODEL), _vec(vec_ref, V_GMEM)).astype(BF16)
    kv = _dot(mem_n, wkv_ref[0].astype(BF16))
    gk = _vec(vec_ref, V_GXK, X_HEAD_DIM)
    for b in range(n):
        rows = slice(b * N_MEM, (b + 1) * N_MEM)
        for hd in range(N_X_HEADS):
            cols = slice(hd * X_HEAD_DIM, (hd + 1) * X_HEAD_DIM)
            kt_ref[0, b, cols, :] = _rms(kv[rows, cols], gk).T.astype(BF16)
        v_ref[0, b] = kv[rows, D_MODEL:].astype(BF16)


def _memkv(mem, p):
    batch = mem.shape[0]
    n_layers = p["wkv_x"].shape[0]

    def by_layer(a):
        return pl.BlockSpec((1,) + a.shape[1:], lambda l, j: (l,) + (0,) * (a.ndim - 1))

    return pl.pallas_call(
        _memkv_kernel,
        grid=(n_layers, batch // MEMKV_BATCHES),
        in_specs=[pl.BlockSpec((MEMKV_BATCHES, N_MEM, D_MODEL), lambda l, j: (j, 0, 0)),
                  by_layer(p["vec"]), by_layer(p["wkv_x"])],
        out_specs=[pl.BlockSpec((1, MEMKV_BATCHES, D_MODEL, N_MEM), lambda l, j: (l, j, 0, 0)),
                   pl.BlockSpec((1, MEMKV_BATCHES, N_MEM, D_MODEL), lambda l, j: (l, j, 0, 0))],
        out_shape=[jax.ShapeDtypeStruct((n_layers, batch, D_MODEL, N_MEM), BF16),
                   jax.ShapeDtypeStruct((n_layers, batch, N_MEM, D_MODEL), BF16)],
        compiler_params=pltpu.CompilerParams(
            dimension_semantics=("arbitrary", "arbitrary"), vmem_limit_bytes=VMEM_LIMIT_BYTES),
        name="memkv",
    )(mem, p["vec"], p["wkv_x"])


def _swa_bias(sinks):
    qi = np.arange(BLOCK)[:, None]
    sk = np.arange(2 * BLOCK)[None, :]
    dist = qi + BLOCK - sk
    valid = (dist >= 0) & (dist < BLOCK)
    assert not valid[:, 0].any()
    slopes = 2.0 ** (-8.0 * (np.arange(N_Q_HEADS, dtype=np.float32) + 1.0) / N_Q_HEADS)
    table = np.empty((2, Q_PER_KV * BLOCK, N_KV_HEADS * 2 * BLOCK), np.float32)
    for first in range(2):
        ok = valid & (sk >= BLOCK) if first else valid
        for j in range(Q_PER_KV):
            for c in range(N_KV_HEADS):
                slope = slopes[j + Q_PER_KV * c]
                blk = np.where(ok, -slope * LOG2E * dist.astype(np.float32), np.float32(NEG))
                table[first, j * BLOCK:(j + 1) * BLOCK, c * 2 * BLOCK:(c + 1) * 2 * BLOCK] = blk
    sink_col = np.zeros(table.shape[-1], bool)
    sink_col[::2 * BLOCK] = True
    n_layers = sinks.shape[0]
    logit = sinks.astype(F32).reshape(n_layers, N_KV_HEADS, Q_PER_KV) * LOG2E
    logit = jnp.repeat(jnp.repeat(jnp.swapaxes(logit, 1, 2), BLOCK, axis=1), 2 * BLOCK, axis=2)
    return jnp.where(jnp.asarray(sink_col), logit[:, None], jnp.asarray(table)[None])


def _prepare_params(norm_mix_g, w_in, q_norm_g, k_norm_g, sinks, conv_w, conv_b, conv_ln_g, conv_ln_b, w_out,
                    norm_x_g, norm_mem_g, wq_x, wkv_x, xq_norm_g, xk_norm_g, wo_x, norm_ffn_g, w_gate_up, w_down):
    n_layers = w_in.shape[0]

    def padded(a):
        a = a.astype(F32)
        return jnp.pad(a, ((0, 0), (0, D_MODEL - a.shape[1])))

    rows = [None] * N_VEC_ROWS
    rows[V_GMIX] = padded(norm_mix_g)
    rows[V_GX] = padded(norm_x_g)
    rows[V_GFFN] = padded(norm_ffn_g)
    rows[V_GQ] = padded(jnp.tile(q_norm_g, (1, N_Q_HEADS)) * (HEAD_DIM ** -0.5 * LOG2E))
    rows[V_GK] = padded(jnp.tile(k_norm_g, (1, N_KV_HEADS)))
    rows[V_CB] = padded(conv_b)
    rows[V_LNG] = padded(conv_ln_g)
    rows[V_LNB] = padded(conv_ln_b)
    rows[V_GXQ] = padded(xq_norm_g * (X_HEAD_DIM ** -0.5 * LOG2E))
    rows[V_GMEM] = padded(norm_mem_g)
    rows[V_GXK] = padded(xk_norm_g)
    zero_row = jnp.zeros((n_layers, D_MODEL), F32)
    vec = jnp.stack([zero_row if r is None else r for r in rows], axis=1)

    return dict(
        vec=vec,
        bias=_swa_bias(sinks),
        cw=jnp.broadcast_to(conv_w.astype(F32)[:, :, None, :], (n_layers, CONV_K, SUBLANES, CONV_CH)),
        w_in=w_in, w_out=w_out, wq_x=wq_x, wkv_x=wkv_x, wo_x=wo_x, w_gate_up=w_gate_up, w_down=w_down,
    )


def kernel(x, mem, norm_mix_g, w_in, q_norm_g, k_norm_g, sinks, conv_w, conv_b, conv_ln_g, conv_ln_b, w_out, norm_x_g, norm_mem_g, wq_x, wkv_x, xq_norm_g, xk_norm_g, wo_x, norm_ffn_g, w_gate_up, w_down):
    p = _prepare_params(norm_mix_g, w_in, q_norm_g, k_norm_g, sinks, conv_w, conv_b, conv_ln_g, conv_ln_b, w_out,
                        norm_x_g, norm_mem_g, wq_x, wkv_x, xq_norm_g, xk_norm_g, wo_x, norm_ffn_g, w_gate_up,
                        w_down)
    kt, v = _memkv(mem, p)
    for layer in range(w_in.shape[0]):
        x = _layer(x, kt, v, p, layer)
    return x
```
